```python
import math
import jax, jax.numpy as jnp
from jax import lax
import numpy as np

D_MODEL = 1024
BATCH = 8
SEQ = 2048
DEPTH = 1
DEC_BATCH = 16
DEC_SEQ = 64
PAST_LEN = 2048

CHUNK = 64
Q_BLOCK = 128
HG_KEY_DIM = 128
N_HG_HEADS = D_MODEL // HG_KEY_DIM
HG_VAL_DIM = D_MODEL // N_HG_HEADS
DA_HEAD_DIM = 64
N_DA_HEADS = D_MODEL // (2 * DA_HEAD_DIM)
DA_V_DIM = 2 * DA_HEAD_DIM
D_FF = 128 * ((8 * D_MODEL // 3 + 127) // 128)
CONV_WIDTH = 3
LN_EPS = 1e-5
DEEPNORM_ALPHA = (2 * DEPTH) ** 0.25
DEEPNORM_BETA = (8 * DEPTH) ** -0.25
DA_SCALE = DA_HEAD_DIM ** -0.5

HG_QK = N_HG_HEADS * HG_KEY_DIM
HG_V = N_HG_HEADS * HG_VAL_DIM
DA_QK = N_DA_HEADS * 2 * DA_HEAD_DIM
DA_V = N_DA_HEADS * DA_V_DIM
IN_SPLITS = (HG_QK, HG_QK, HG_V, HG_V, DA_QK, DA_QK, DA_V, D_MODEL, D_MODEL)
IN_WIDTH = sum(IN_SPLITS)

kernel_name = "hgrn2_diffattn_streaming_encoder_step"


def _layer_norm(x, g, b):
    xf = x.astype(jnp.float32)
    mu = jnp.mean(xf, axis=-1, keepdims=True)
    var = jnp.mean(jnp.square(xf - mu), axis=-1, keepdims=True)
    return ((xf - mu) * lax.rsqrt(var + LN_EPS) * g + b).astype(x.dtype)


def _rms_norm(x, g):
    xf = x.astype(jnp.float32)
    return (xf * lax.rsqrt(jnp.mean(jnp.square(xf), axis=-1, keepdims=True) + LN_EPS) * g).astype(x.dtype)


def _split_in(u):
    outs, off = [], 0
    for w in IN_SPLITS:
        outs.append(u[..., off:off + w])
        off += w
    return outs


def _hgrn2(q, k, v, logf, s0):
    B, T, H, K = q.shape
    V = v.shape[-1]
    C = min(CHUNK, T)
    n = T // C

    def to_chunks(a):
        return a.reshape(B, n, C, H, a.shape[-1]).transpose(1, 0, 3, 2, 4)

    causal = jnp.tril(jnp.ones((C, C), dtype=bool))[:, :, None]

    def step(S, inp):
        qc, kc, vc, gc = inp
        b = jnp.cumsum(gc, axis=2)
        diff = b[:, :, :, None, :] - b[:, :, None, :, :]
        dec = jnp.exp(jnp.where(causal, diff, -jnp.inf))
        att = jnp.einsum('bhtk,bhtsk,bhsk->bhts', qc, dec, kc)
        o = jnp.einsum('bhts,bhsv->bhtv', att, vc) + jnp.einsum('bhtk,bhkv->bhtv', qc * jnp.exp(b), S)
        b_last = b[:, :, -1]
        S_new = jnp.exp(b_last)[..., None] * S + jnp.einsum(
            'bhsk,bhsv->bhkv', kc * jnp.exp(b_last[:, :, None, :] - b), vc)
        return S_new, o

    S, o = lax.scan(step, s0, (to_chunks(q), to_chunks(k), to_chunks(v), to_chunks(logf)))
    o = o.transpose(1, 0, 3, 2, 4).reshape(B, T, H, V)
    return S, o


def _diff_attend(q, k, v, lam, mask):
    s = jnp.einsum('bqhcd,bshcd->bhcqs', q.astype(jnp.float32), k.astype(jnp.float32)) * DA_SCALE
    s = jnp.where(mask, s, -jnp.inf)
    p = jax.nn.softmax(s, axis=-1)
    w = p[:, :, 0] - lam * p[:, :, 1]
    return jnp.einsum('bhqs,bshe->bqhe', w.astype(v.dtype), v)


def _diff_attn_prompt(q, k, v, lam):
    B, T = q.shape[:2]
    nb = T // Q_BLOCK
    qb = q.reshape(B, nb, Q_BLOCK, N_DA_HEADS, 2, DA_HEAD_DIM).transpose(1, 0, 2, 3, 4, 5)
    k_chunk = jnp.arange(T) // CHUNK

    def block(args):
        qi, i = args
        q_chunk = (i * Q_BLOCK + jnp.arange(Q_BLOCK)) // CHUNK
        mask = k_chunk[None, :] <= q_chunk[:, None]
        return _diff_attend(qi, k, v, lam, mask)

    o = lax.map(block, (qb, jnp.arange(nb)))
    return o.transpose(1, 0, 2, 3, 4).reshape(B, T, N_DA_HEADS, DA_V_DIM)


def _layer(x, l, lb, hg_s0, past_k, past_v, ffn_buf0, w_in, hg_norm_g, lq1, lk1, lq2, lk2,
           subln_g, w_br_hg, w_br_da, w_out, ln1_g, ln1_b, w_up, conv_w, conv_b, w_down,
           ln2_g, ln2_b):
    B, T, _ = x.shape
    hq, hf, hi, hog, dq, dk, dv, g_hg, g_da = _split_in(x @ w_in)

    lb_hk = lb.reshape(N_HG_HEADS, HG_KEY_DIM)
    q = hq.reshape(B, T, N_HG_HEADS, HG_KEY_DIM).astype(jnp.float32)
    f = lb_hk + (1.0 - lb_hk) * jax.nn.sigmoid(hf.reshape(B, T, N_HG_HEADS, HG_KEY_DIM).astype(jnp.float32))
    logf = jnp.log(f)
    kk = 1.0 - f
    vv = hi.reshape(B, T, N_HG_HEADS, HG_VAL_DIM).astype(jnp.float32)
    s_new, o = _hgrn2(q, kk, vv, logf, hg_s0.astype(jnp.float32))
    o = _rms_norm(o, hg_norm_g.astype(jnp.float32)) * jax.nn.sigmoid(
        hog.reshape(B, T, N_HG_HEADS, HG_VAL_DIM).astype(jnp.float32))
    y_hg = o.reshape(B, T, HG_V).astype(x.dtype)

    lam_init = 0.8 - 0.6 * math.exp(-0.3 * l)
    lam = (jnp.exp(jnp.sum(lq1.astype(jnp.float32) * lk1.astype(jnp.float32)))
           - jnp.exp(jnp.sum(lq2.astype(jnp.float32) * lk2.astype(jnp.float32))) + lam_init)
    q5 = dq.reshape(B, T, N_DA_HEADS, 2, DA_HEAD_DIM)
    k5 = dk.reshape(B, T, N_DA_HEADS, 2, DA_HEAD_DIM)
    v4 = dv.reshape(B, T, N_DA_HEADS, DA_V_DIM)
    if past_k is None:
        a = _diff_attn_prompt(q5, k5, v4, lam)
    else:
        P = past_k.shape[1]
        k_all = jnp.concatenate([past_k.reshape(B, P, N_DA_HEADS, 2, DA_HEAD_DIM).astype(k5.dtype), k5], axis=1)
        v_all = jnp.concatenate([past_v.astype(v4.dtype), v4], axis=1)
        mask = jnp.ones((T, P + T), dtype=bool)
        a = _diff_attend(q5, k_all, v_all, lam, mask)
    a = _rms_norm(a, subln_g) * (1.0 - lam_init)
    y_da = a.reshape(B, T, DA_V)

    m = jax.nn.sigmoid(g_hg) * (y_hg @ w_br_hg) + jax.nn.sigmoid(g_da) * (y_da @ w_br_da)
    h = _layer_norm(DEEPNORM_ALPHA * x + m @ w_out, ln1_g, ln1_b)

    up = h @ w_up
    buf = jnp.concatenate([ffn_buf0.astype(up.dtype), up], axis=1)
    c = conv_b
    for j in range(CONV_WIDTH):
        c = c + conv_w[j] * buf[:, j:j + T]
    val, gate = jnp.split(c, 2, axis=-1)
    ffn = (jax.nn.silu(gate) * val) @ w_down
    out = _layer_norm(DEEPNORM_ALPHA * h + ffn, ln2_g, ln2_b)

    new_k = k5.reshape(B, T, N_DA_HEADS, 2 * DA_HEAD_DIM)
    return out, new_k, v4, s_new.astype(x.dtype), buf[:, -(CONV_WIDTH - 1):]


def setup_inputs(seed: int = 0) -> dict:
    key = jax.random.key(seed)
    ks = jax.random.split(key, 26)

    def nrm(k, shape, scale):
        return jax.random.normal(k, shape, jnp.float32) * scale

    off_hi = 2 * HG_QK
    off_dv = 2 * HG_QK + 2 * HG_V + 2 * DA_QK
    col_scale = jnp.concatenate([
        jnp.ones((off_hi,), jnp.float32), jnp.full((HG_V,), DEEPNORM_BETA, jnp.float32),
        jnp.ones((HG_V + 2 * DA_QK,), jnp.float32), jnp.full((DA_V,), DEEPNORM_BETA, jnp.float32),
        jnp.ones((2 * D_MODEL,), jnp.float32)])
    assert off_dv == off_hi + 2 * HG_V + 2 * DA_QK - HG_V + HG_V
    return {
        "x_prompt": nrm(ks[0], (BATCH, SEQ, D_MODEL), 1.0),
        "x_sample": nrm(ks[1], (DEC_BATCH, DEC_SEQ, D_MODEL), 1.0),
        "cache_k": nrm(ks[2], (DEPTH, DEC_BATCH, PAST_LEN, N_DA_HEADS, 2 * DA_HEAD_DIM), 1.0),
        "cache_v": nrm(ks[3], (DEPTH, DEC_BATCH, PAST_LEN, N_DA_HEADS, DA_V_DIM), DEEPNORM_BETA),
        "state_hgrn": nrm(ks[4], (DEPTH, DEC_BATCH, N_HG_HEADS, HG_KEY_DIM, HG_VAL_DIM), 0.5),
        "state_ffn_conv": nrm(ks[5], (DEPTH, DEC_BATCH, CONV_WIDTH - 1, 2 * D_FF), 0.5),
        "w_in": nrm(ks[6], (DEPTH, D_MODEL, IN_WIDTH), D_MODEL ** -0.5) * col_scale,
        "hg_lb_logits": nrm(ks[7], (DEPTH + 1, HG_QK), 0.5),
        "hg_norm_g": 1.0 + nrm(ks[8], (DEPTH, HG_VAL_DIM), 0.02),
        "da_lambda_q1": nrm(ks[9], (DEPTH, DA_HEAD_DIM), 0.1),
        "da_lambda_k1": nrm(ks[10], (DEPTH, DA_HEAD_DIM), 0.1),
        "da_lambda_q2": nrm(ks[11], (DEPTH, DA_HEAD_DIM), 0.1),
        "da_lambda_k2": nrm(ks[12], (DEPTH, DA_HEAD_DIM), 0.1),
        "da_subln_g": 1.0 + nrm(ks[13], (DEPTH, DA_V_DIM), 0.02),
        "w_br_hg": nrm(ks[14], (DEPTH, HG_V, D_MODEL), HG_V ** -0.5 * DEEPNORM_BETA),
        "w_br_da": nrm(ks[15], (DEPTH, DA_V, D_MODEL), DA_V ** -0.5 * DEEPNORM_BETA),
        "w_out": nrm(ks[16], (DEPTH, D_MODEL, D_MODEL), D_MODEL ** -0.5 * DEEPNORM_BETA),
        "ln1_g": 1.0 + nrm(ks[17], (DEPTH, D_MODEL), 0.02),
        "ln1_b": nrm(ks[18], (DEPTH, D_MODEL), 0.02),
        "w_up": nrm(ks[19], (DEPTH, D_MODEL, 2 * D_FF), D_MODEL ** -0.5 * DEEPNORM_BETA),
        "conv_w": nrm(ks[20], (DEPTH, CONV_WIDTH, 2 * D_FF), CONV_WIDTH ** -0.5),
        "conv_b": nrm(ks[21], (DEPTH, 2 * D_FF), 0.02),
        "w_down": nrm(ks[22], (DEPTH, D_FF, D_MODEL), D_FF ** -0.5 * DEEPNORM_BETA),
        "ln2_g": 1.0 + nrm(ks[23], (DEPTH, D_MODEL), 0.02),
        "ln2_b": nrm(ks[24], (DEPTH, D_MODEL), 0.02),
    }


def reference(x_prompt, x_sample, cache_k, cache_v, state_hgrn, state_ffn_conv, w_in, hg_lb_logits,
              hg_norm_g, da_lambda_q1, da_lambda_k1, da_lambda_q2, da_lambda_k2, da_subln_g,
              w_br_hg, w_br_da, w_out, ln1_g, ln1_b, w_up, conv_w, conv_b, w_down, ln2_g, ln2_b):
    lb_all = jnp.cumsum(jax.nn.softmax(hg_lb_logits.astype(jnp.float32), axis=0), axis=0)[:DEPTH]
    xp, xs = x_prompt, x_sample
    kp, vp, sp, cp, kd, vd, sd, cd = [], [], [], [], [], [], [], []
    for l in range(DEPTH):
        w = (w_in[l], hg_norm_g[l], da_lambda_q1[l], da_lambda_k1[l], da_lambda_q2[l], da_lambda_k2[l],
             da_subln_g[l], w_br_hg[l], w_br_da[l], w_out[l], ln1_g[l], ln1_b[l], w_up[l], conv_w[l],
             conv_b[l], w_down[l], ln2_g[l], ln2_b[l])
        hg0 = jnp.zeros((xp.shape[0], N_HG_HEADS, HG_KEY_DIM, HG_VAL_DIM), jnp.float32)
        buf0 = jnp.zeros((xp.shape[0], CONV_WIDTH - 1, 2 * D_FF), xp.dtype)
        xp, k1, v1, s1, c1 = _layer(xp, l, lb_all[l], hg0, None, None, buf0, *w)
        xs, k2, v2, s2, c2 = _layer(xs, l, lb_all[l], state_hgrn[l], cache_k[l], cache_v[l],
                                    state_ffn_conv[l], *w)
        kp.append(k1); vp.append(v1); sp.append(s1); cp.append(c1)
        kd.append(k2); vd.append(v2); sd.append(s2); cd.append(c2)
    k_prompt = jnp.stack(kp); v_prompt = jnp.stack(vp)
    hgrn_prompt = jnp.stack(sp); conv_prompt = jnp.stack(cp)
    k_sample = jnp.stack(kd); v_sample = jnp.stack(vd)
    hgrn_sample = jnp.stack(sd); conv_sample = jnp.stack(cd)
    return (xp, xs, k_prompt, v_prompt, hgrn_prompt, conv_prompt, k_sample, v_sample, hgrn_sample, conv_sample)
```

```python
import functools
import math

import jax
import jax.numpy as jnp
from jax import lax
from jax.experimental import pallas as pl
from jax.experimental.pallas import tpu as pltpu

F32 = jnp.float32
BF16 = jnp.bfloat16

D_MODEL = 1024
DEPTH = 1
CHUNK = 64
HG_DIM = 128
N_HEADS = D_MODEL // HG_DIM
DA_HEAD_DIM = 64
D_FF = 128 * ((8 * D_MODEL // 3 + 127) // 128)
CONV_WIDTH = 3
LN_EPS = 1e-5
ALPHA = (2 * DEPTH) ** 0.25
DA_SCALE = DA_HEAD_DIM ** -0.5
N_IN_GROUPS = 9
NEG = -1e30

VMEM_LIMIT = 56 * 1024 * 1024

HG_C = 64
HG_R = 16
FF_CHUNK = D_FF // 2


def _params(n_axes):
    return pltpu.CompilerParams(dimension_semantics=("arbitrary",) * n_axes,
                                vmem_limit_bytes=VMEM_LIMIT)


def _resident(shape):
    nd = len(shape)
    return pl.BlockSpec(shape, lambda *_: (0,) * nd, pipeline_mode=pl.Buffered(1))


def _sigmoid(x):
    return 1.0 / (1.0 + jnp.exp(-x))


def _layer_norm(z, g, b):
    mu = jnp.mean(z, axis=-1, keepdims=True)
    zc = z - mu
    var = jnp.mean(zc * zc, axis=-1, keepdims=True)
    return zc * lax.rsqrt(var + LN_EPS) * g + b


def _in_proj_kernel(x_ref, w_ref, *out_refs):
    xb = x_ref[...].astype(BF16)
    for j, o_ref in enumerate(out_refs):
        o_ref[...] = jnp.dot(xb, w_ref[:, j * D_MODEL:(j + 1) * D_MODEL],
                             preferred_element_type=F32)


def _in_proj(xf, w_in_b, tm):
    n = xf.shape[0]
    tile = pl.BlockSpec((tm, D_MODEL), lambda i: (i, 0))
    return pl.pallas_call(
        _in_proj_kernel,
        grid=(n // tm,),
        in_specs=[tile, _resident((D_MODEL, N_IN_GROUPS * D_MODEL))],
        out_specs=[tile] * N_IN_GROUPS,
        out_shape=[jax.ShapeDtypeStruct((n, D_MODEL), F32)] * N_IN_GROUPS,
        compiler_params=_params(1),
        name="in_proj",
    )(xf, w_in_b)


def _cumsum_rows(g, row):
    b = g
    sh = 1
    while sh < g.shape[0]:
        b = b + jnp.where(row >= sh, pltpu.roll(b, sh, 0), 0.0)
        sh *= 2
    return b


def _hgrn_kernel(lbl_ref, gn_ref, q_ref, f_ref, v_ref, og_ref, s0_ref, y_ref, sout_ref, st_ref,
                 *, n_sub):
    t = pl.program_id(2)

    @pl.when(t == 0)
    def _init():
        st_ref[...] = s0_ref[...].T

    lg = lbl_ref[...]
    e = jnp.exp(lg - jnp.max(lg, axis=0, keepdims=True))
    lb = e[0:1, :] / jnp.sum(e, axis=0, keepdims=True)
    gn = gn_ref[...]

    row = lax.broadcasted_iota(jnp.int32, (HG_C, HG_DIM), 0)
    lane_rc = lax.broadcasted_iota(jnp.int32, (HG_R, HG_C), 1)
    trow_rc = lax.broadcasted_iota(jnp.int32, (HG_R, HG_C), 0)
    nt = (((1,), (1,)), ((), ()))
    tn = (((0,), (0,)), ((), ()))

    def chunk(c, carry):
        r = pl.multiple_of(c * HG_C, HG_C)
        q = q_ref[pl.ds(r, HG_C), :]
        v = v_ref[pl.ds(r, HG_C), :]
        f = lb + (1.0 - lb) * _sigmoid(f_ref[pl.ds(r, HG_C), :])
        g = jnp.log(f)
        kk = 1.0 - f
        b = _cumsum_rows(g, row)
        b_ex = jnp.where(row == 0, 0.0, pltpu.roll(b, 1, 0))
        st = st_ref[...]

        blocks = []
        for i in range(HG_C // HG_R):
            r0 = i * HG_R
            bi = b[r0:r0 + HG_R]
            qi = q[r0:r0 + HG_R]
            b0 = b_ex[r0:r0 + 1]
            qt = qi * jnp.exp(bi - b0)
            kt = kk * jnp.exp(jnp.minimum(b0 - b, 0.0))
            a_off = lax.dot_general(qt.astype(BF16), kt.astype(BF16), nt,
                                    preferred_element_type=F32)
            diag = jnp.zeros((HG_R, HG_C), F32)
            for sl in range(HG_R):
                s = r0 + sl
                p = qi * jnp.exp(jnp.minimum(bi - b[s:s + 1], 0.0)) * kk[s:s + 1]
                diag = jnp.where(lane_rc == s, jnp.sum(p, axis=1, keepdims=True), diag)
            blocks.append(jnp.where(lane_rc < r0, a_off,
                                    jnp.where(lane_rc <= r0 + trow_rc, diag, 0.0)))
        a = jnp.concatenate(blocks, axis=0)

        qh = q * jnp.exp(b)
        o = (jnp.dot(a.astype(BF16), v.astype(BF16), preferred_element_type=F32)
             + lax.dot_general(qh.astype(BF16), st.astype(BF16), nt, preferred_element_type=F32))
        bl = b[HG_C - 1:HG_C]
        kh = kk * jnp.exp(bl - b)
        st_ref[...] = st * jnp.exp(bl) + lax.dot_general(
            v.astype(BF16), kh.astype(BF16), tn, preferred_element_type=F32)

        ms = jnp.mean(o * o, axis=1, keepdims=True)
        y_ref[pl.ds(r, HG_C), :] = (o * lax.rsqrt(ms + LN_EPS) * gn
                                    * _sigmoid(og_ref[pl.ds(r, HG_C), :]))
        return carry

    lax.fori_loop(0, n_sub, chunk, 0)

    @pl.when(t == pl.num_programs(2) - 1)
    def _final():
        sout_ref[...] = st_ref[...].T


def _hgrn(lb_logits, gn, hq, hf, hi, hog, s0, tc):
    bsz, t_len, _ = hq.shape
    tok = pl.BlockSpec((None, tc, HG_DIM), lambda b, h, t: (b, t, h))
    state = pl.BlockSpec((None, None, HG_DIM, HG_DIM), lambda b, h, t: (b, h, 0, 0))
    return pl.pallas_call(
        functools.partial(_hgrn_kernel, n_sub=tc // HG_C),
        grid=(bsz, N_HEADS, t_len // tc),
        in_specs=[pl.BlockSpec((DEPTH + 1, HG_DIM), lambda b, h, t: (0, h)),
                  pl.BlockSpec((1, HG_DIM), lambda b, h, t: (0, 0)),
                  tok, tok, tok, tok, state],
        out_specs=[tok, state],
        out_shape=[jax.ShapeDtypeStruct(hq.shape, F32),
                   jax.ShapeDtypeStruct(s0.shape, F32)],
        scratch_shapes=[pltpu.VMEM((HG_DIM, HG_DIM), F32)],
        compiler_params=_params(3),
        name="hgrn",
    )(lb_logits, gn, hq, hf, hi, hog, s0)


def _attn_kernel(*refs, tq, tq_pad, tk, t_new, n_past, causal, lam_init):
    lq1_ref, lk1_ref, lq2_ref, lk2_ref, g_ref, q_ref, kn_ref, vn_ref = refs[:8]
    pos = 8
    if n_past:
        kp_ref, vp_ref = refs[pos:pos + 2]
        pos += 2
    y_ref = refs[pos]
    kb_ref, vt_ref, m_ref, l_ref, acc_ref = refs[pos + 1:pos + 6]
    if n_past:
        kpb_ref, vpt_ref = refs[pos + 6:pos + 8]
    qi = pl.program_id(2)
    nt = (((1,), (1,)), ((), ()))
    tn_pad = kb_ref.shape[0]

    @pl.when(qi == 0)
    def _stage():
        k = kn_ref[...]
        v = vn_ref[...]
        if tn_pad > t_new:
            zpad = jnp.zeros((tn_pad - t_new, HG_DIM), F32)
            k = jnp.concatenate([k, zpad], axis=0)
            v = jnp.concatenate([v, zpad], axis=0)
        kb_ref[...] = k.astype(BF16)
        vt_ref[...] = v.T.astype(BF16)
        if n_past:
            kpb_ref[...] = kp_ref[...].astype(BF16)
            vpt_ref[...] = vp_ref[...].T.astype(BF16)

    lam = (jnp.exp(jnp.sum(lq1_ref[...] * lk1_ref[...], axis=1, keepdims=True))
           - jnp.exp(jnp.sum(lq2_ref[...] * lk2_ref[...], axis=1, keepdims=True)) + lam_init)

    q = q_ref[...] * DA_SCALE
    if tq_pad > tq:
        q = jnp.concatenate([q, jnp.zeros((tq_pad - tq, HG_DIM), F32)], axis=0)
    lane = lax.broadcasted_iota(jnp.int32, (tq_pad, HG_DIM), 1)
    qm = (jnp.where(lane < DA_HEAD_DIM, q, 0.0).astype(BF16),
          jnp.where(lane >= DA_HEAD_DIM, q, 0.0).astype(BF16))

    m_ref[...] = jnp.full(m_ref.shape, NEG, F32)
    l_ref[...] = jnp.zeros(l_ref.shape, F32)
    acc_ref[...] = jnp.zeros(acc_ref.shape, F32)

    def kv_step(kb, vt, mask):
        for mp in range(2):
            s = lax.dot_general(kb, qm[mp], nt, preferred_element_type=F32)
            if mask is not None:
                s = jnp.where(mask, s, NEG)
            m_old = m_ref[mp]
            m_new = jnp.maximum(m_old, jnp.max(s, axis=0, keepdims=True))
            alpha = jnp.exp(m_old - m_new)
            p = jnp.exp(s - m_new)
            l_ref[mp] = alpha * l_ref[mp] + jnp.sum(p, axis=0, keepdims=True)
            acc_ref[mp] = alpha * acc_ref[mp] + jnp.dot(vt, p.astype(BF16),
                                                        preferred_element_type=F32)
            m_ref[mp] = m_new

    if n_past:
        def past_step(j, carry):
            c0 = pl.multiple_of(j * tk, tk)
            kv_step(kpb_ref[pl.ds(c0, tk), :], vpt_ref[:, pl.ds(c0, tk)], None)
            return carry
        lax.fori_loop(0, n_past, past_step, 0)

    if causal:
        def full_step(j, carry):
            c0 = pl.multiple_of(j * tk, tk)
            kv_step(kb_ref[pl.ds(c0, tk), :], vt_ref[:, pl.ds(c0, tk)], None)
            return carry
        lax.fori_loop(0, qi, full_step, 0)
        c0 = pl.multiple_of(qi * tk, tk)
        key_chunk = lax.broadcasted_iota(jnp.int32, (tk, tq_pad), 0) // CHUNK
        qry_chunk = lax.broadcasted_iota(jnp.int32, (tk, tq_pad), 1) // CHUNK
        kv_step(kb_ref[pl.ds(c0, tk), :], vt_ref[:, pl.ds(c0, tk)], key_chunk <= qry_chunk)
    else:
        mask = None
        if tn_pad > t_new:
            mask = lax.broadcasted_iota(jnp.int32, (tn_pad, tq_pad), 0) < t_new
        kv_step(kb_ref[...], vt_ref[...], mask)

    a = acc_ref[0] / l_ref[0] - lam * (acc_ref[1] / l_ref[1])
    ms = jnp.mean(a * a, axis=0, keepdims=True)
    y = a * lax.rsqrt(ms + LN_EPS) * g_ref[...] * (1.0 - lam_init)
    y_ref[...] = y.T[:tq]


def _attn(lams, g_col, dq, dk, dv, past_k, past_v, lam_init):
    bsz, t_len, _ = dq.shape
    causal = past_k is None
    if causal:
        tq = tk = 256
        tq_pad, tn_pad, n_past = tq, t_len, 0
    else:
        tq, tk = t_len, 256
        tq_pad = tn_pad = 128
        n_past = past_k.shape[1] // tk
    vec = pl.BlockSpec((1, DA_HEAD_DIM), lambda b, h, i: (0, 0))
    qblk = pl.BlockSpec((None, tq, HG_DIM), lambda b, h, i: (b, i, h))
    seq = pl.BlockSpec((None, t_len, HG_DIM), lambda b, h, i: (b, 0, h))
    in_specs = [vec, vec, vec, vec, pl.BlockSpec((HG_DIM, 1), lambda b, h, i: (0, 0)),
                qblk, seq, seq]
    args = list(lams) + [g_col, dq, dk, dv]
    scratch = [pltpu.VMEM((tn_pad, HG_DIM), BF16), pltpu.VMEM((HG_DIM, tn_pad), BF16),
               pltpu.VMEM((2, 1, tq_pad), F32), pltpu.VMEM((2, 1, tq_pad), F32),
               pltpu.VMEM((2, HG_DIM, tq_pad), F32)]
    if not causal:
        p_len = past_k.shape[1]
        pseq = pl.BlockSpec((None, p_len, HG_DIM), lambda b, h, i: (b, 0, h))
        in_specs += [pseq, pseq]
        args += [past_k, past_v]
        scratch += [pltpu.VMEM((p_len, HG_DIM), BF16), pltpu.VMEM((HG_DIM, p_len), BF16)]
    return pl.pallas_call(
        functools.partial(_attn_kernel, tq=tq, tq_pad=tq_pad, tk=tk, t_new=t_len,
                          n_past=n_past, causal=causal, lam_init=lam_init),
        grid=(bsz, N_HEADS, t_len // tq),
        in_specs=in_specs,
        out_specs=qblk,
        out_shape=jax.ShapeDtypeStruct(dq.shape, F32),
        scratch_shapes=scratch,
        compiler_params=_params(3),
        name="attn_prompt" if causal else "attn_sample",
    )(*args)


def _merge_kernel(x_ref, yh_ref, yd_ref, gh_ref, gd_ref, wh_ref, wd_ref, wo_ref, g_ref, b_ref, h_ref):
    mh = jnp.dot(yh_ref[...].astype(BF16), wh_ref[...], preferred_element_type=F32)
    md = jnp.dot(yd_ref[...].astype(BF16), wd_ref[...], preferred_element_type=F32)
    m = _sigmoid(gh_ref[...]) * mh + _sigmoid(gd_ref[...]) * md
    z = ALPHA * x_ref[...] + jnp.dot(m.astype(BF16), wo_ref[...], preferred_element_type=F32)
    h_ref[...] = _layer_norm(z, g_ref[...], b_ref[...])


def _merge(xf, yh, yd, gh, gd, wh, wd, wo, g, b, tm):
    n = xf.shape[0]
    tile = pl.BlockSpec((tm, D_MODEL), lambda i: (i, 0))
    wspec = _resident((D_MODEL, D_MODEL))
    vspec = _resident((1, D_MODEL))
    return pl.pallas_call(
        _merge_kernel,
        grid=(n // tm,),
        in_specs=[tile] * 5 + [wspec] * 3 + [vspec] * 2,
        out_specs=tile,
        out_shape=jax.ShapeDtypeStruct((n, D_MODEL), F32),
        compiler_params=_params(1),
        name="merge",
    )(xf, yh, yd, gh, gd, wh, wd, wo, g, b)


def _ffn_kernel(h_ref, buf0_ref, wup_ref, cw_ref, cb_ref, wdn_ref, g_ref, b_ref,
                out_ref, cst_ref, prev_ref, *, tm):
    t = pl.program_id(1)

    @pl.when(t == 0)
    def _init():
        prev_ref[...] = buf0_ref[...]

    h = h_ref[...]
    hb = h.astype(BF16)
    row = lax.broadcasted_iota(jnp.int32, (tm, FF_CHUNK), 0)

    def conv_cols(c0):
        cols = slice(c0, c0 + FF_CHUNK)
        up = jnp.dot(hb, wup_ref[:, cols], preferred_element_type=F32)
        pv = prev_ref[:, cols]
        up1 = jnp.where(row == 0, pv[1:2], pltpu.roll(up, 1, 0))
        up2 = jnp.where(row == 0, pv[0:1], jnp.where(row == 1, pv[1:2], pltpu.roll(up, 2, 0)))
        last = up[tm - (CONV_WIDTH - 1):tm]
        prev_ref[:, cols] = last
        cst_ref[:, cols] = last
        cw = cw_ref[:, cols]
        return cb_ref[:, cols] + cw[0:1] * up2 + cw[1:2] * up1 + cw[2:3] * up

    acc = jnp.zeros((tm, D_MODEL), F32)
    for c in range(D_FF // FF_CHUNK):
        val = conv_cols(c * FF_CHUNK)
        gate = conv_cols(D_FF + c * FF_CHUNK)
        act = gate * _sigmoid(gate) * val
        acc = acc + jnp.dot(act.astype(BF16), wdn_ref[c * FF_CHUNK:(c + 1) * FF_CHUNK, :],
                            preferred_element_type=F32)
    out_ref[...] = _layer_norm(ALPHA * h + acc, g_ref[...], b_ref[...])


def _ffn(h3, buf0, wup, cw, cb, wdn, g, b, tm):
    bsz, t_len, _ = h3.shape
    tile = pl.BlockSpec((None, tm, D_MODEL), lambda bi, t: (bi, t, 0))
    cst = pl.BlockSpec((None, CONV_WIDTH - 1, 2 * D_FF), lambda bi, t: (bi, 0, 0))
    return pl.pallas_call(
        functools.partial(_ffn_kernel, tm=tm),
        grid=(bsz, t_len // tm),
        in_specs=[tile, cst, _resident((D_MODEL, 2 * D_FF)), _resident((CONV_WIDTH, 2 * D_FF)),
                  _resident((1, 2 * D_FF)), _resident((D_FF, D_MODEL)),
                  _resident((1, D_MODEL)), _resident((1, D_MODEL))],
        out_specs=[tile, cst],
        out_shape=[jax.ShapeDtypeStruct(h3.shape, F32),
                   jax.ShapeDtypeStruct(buf0.shape, F32)],
        scratch_shapes=[pltpu.VMEM((CONV_WIDTH - 1, 2 * D_FF), F32)],
        compiler_params=_params(2),
        name="ffn",
    )(h3, buf0, wup, cw, cb, wdn, g, b)


def _pick_tile(n, want):
    tm = min(n, want)
    assert n % tm == 0, (n, tm)
    return tm


def _layer(x, layer_idx, w, s0, past_k, past_v, buf0):
    bsz, t_len, _ = x.shape
    n = bsz * t_len
    xf = x.reshape(n, D_MODEL)
    hq, hf, hi, hog, dq, dk, dv, g_hg, g_da = _in_proj(xf, w["w_in"], _pick_tile(n, 256))

    def seq(a):
        return a.reshape(bsz, t_len, D_MODEL)

    y_hg, s_new = _hgrn(w["lb_logits"], w["hg_norm_g"], seq(hq), seq(hf), seq(hi), seq(hog), s0,
                        _pick_tile(t_len, 512))
    lam_init = 0.8 - 0.6 * math.exp(-0.3 * layer_idx)
    if past_k is not None:
        p_len = past_k.shape[1]
        past_k = past_k.reshape(bsz, p_len, D_MODEL)
        past_v = past_v.reshape(bsz, p_len, D_MODEL)
    y_da = _attn(w["lams"], w["subln_g"], seq(dq), seq(dk), seq(dv), past_k, past_v, lam_init)
    h = _merge(xf, y_hg.reshape(n, D_MODEL), y_da.reshape(n, D_MODEL), g_hg, g_da,
               w["w_br_hg"], w["w_br_da"], w["w_out"], w["ln1_g"], w["ln1_b"], _pick_tile(n, 256))
    out, cst = _ffn(seq(h), buf0, w["w_up"], w["conv_w"], w["conv_b"], w["w_down"],
                    w["ln2_g"], w["ln2_b"], _pick_tile(t_len, 256))
    new_k = dk.reshape(bsz, t_len, N_HEADS, 2 * DA_HEAD_DIM)
    new_v = dv.reshape(bsz, t_len, N_HEADS, 2 * DA_HEAD_DIM)
    return out, new_k, new_v, s_new, cst


def kernel(x_prompt, x_sample, cache_k, cache_v, state_hgrn, state_ffn_conv, w_in, hg_lb_logits,
           hg_norm_g, da_lambda_q1, da_lambda_k1, da_lambda_q2, da_lambda_k2, da_subln_g,
           w_br_hg, w_br_da, w_out, ln1_g, ln1_b, w_up, conv_w, conv_b, w_down, ln2_g, ln2_b):
    assert w_in.shape[0] == DEPTH == 1 and hg_lb_logits.shape[0] == DEPTH + 1
    l = 0
    w = dict(
        w_in=w_in[l].astype(BF16),
        lb_logits=hg_lb_logits.astype(F32),
        hg_norm_g=hg_norm_g[l].reshape(1, HG_DIM),
        lams=[a[l].reshape(1, DA_HEAD_DIM)
              for a in (da_lambda_q1, da_lambda_k1, da_lambda_q2, da_lambda_k2)],
        subln_g=da_subln_g[l].reshape(HG_DIM, 1),
        w_br_hg=w_br_hg[l].astype(BF16), w_br_da=w_br_da[l].astype(BF16),
        w_out=w_out[l].astype(BF16),
        ln1_g=ln1_g[l].reshape(1, D_MODEL), ln1_b=ln1_b[l].reshape(1, D_MODEL),
        w_up=w_up[l].astype(BF16), conv_w=conv_w[l], conv_b=conv_b[l].reshape(1, 2 * D_FF),
        w_down=w_down[l].astype(BF16),
        ln2_g=ln2_g[l].reshape(1, D_MODEL), ln2_b=ln2_b[l].reshape(1, D_MODEL),
    )
    bp = x_prompt.shape[0]
    s0_p = jnp.zeros((bp, N_HEADS, HG_DIM, HG_DIM), F32)
    buf0_p = jnp.zeros((bp, CONV_WIDTH - 1, 2 * D_FF), x_prompt.dtype)
    yp, kp, vp, sp, cp = _layer(x_prompt, l, w, s0_p, None, None, buf0_p)
    ys, ks, vs, ss, cs = _layer(x_sample, l, w, state_hgrn[l].astype(F32), cache_k[l], cache_v[l],
                                state_ffn_conv[l])
    return (yp, ys, kp[None], vp[None], sp[None], cp[None], ks[None], vs[None], ss[None], cs[None])
```

```python
import functools
import math

import jax
import jax.numpy as jnp
from jax import lax
from jax.experimental import pallas as pl
from jax.experimental.pallas import tpu as pltpu

F32 = jnp.float32
BF16 = jnp.bfloat16

D_MODEL = 1024
DEPTH = 1
CHUNK = 64
HG_DIM = 128
N_HEADS = D_MODEL // HG_DIM
DA_HEAD_DIM = 64
D_FF = 128 * ((8 * D_MODEL // 3 + 127) // 128)
CONV_WIDTH = 3
LN_EPS = 1e-5
ALPHA = (2 * DEPTH) ** 0.25
DA_SCALE = DA_HEAD_DIM ** -0.5
N_IN_GROUPS = 9
NEG = -1e30

VMEM_LIMIT = 56 * 1024 * 1024

HG_C = 64
HG_R = 16
FF_CHUNK = D_FF // 2
ATT_T = 256


def _params(n_axes):
    return pltpu.CompilerParams(dimension_semantics=("arbitrary",) * n_axes,
                                vmem_limit_bytes=VMEM_LIMIT)


def _resident(shape):
    nd = len(shape)
    return pl.BlockSpec(shape, lambda *_: (0,) * nd, pipeline_mode=pl.Buffered(1))


def _sigmoid(x):
    return 1.0 / (1.0 + jnp.exp(-x))


def _layer_norm(z, g, b):
    mu = jnp.mean(z, axis=-1, keepdims=True)
    zc = z - mu
    var = jnp.mean(zc * zc, axis=-1, keepdims=True)
    return zc * lax.rsqrt(var + LN_EPS) * g + b


NT = (((1,), (1,)), ((), ()))
TN = (((0,), (0,)), ((), ()))


def _in_proj_kernel(x_ref, w_ref, *out_refs):
    xb = x_ref[...].astype(BF16)
    for j, o_ref in enumerate(out_refs):
        o_ref[...] = jnp.dot(xb, w_ref[:, j * D_MODEL:(j + 1) * D_MODEL],
                             preferred_element_type=F32)


def _in_proj(xf, w_in_b, tm):
    n = xf.shape[0]
    tile = pl.BlockSpec((tm, D_MODEL), lambda i: (i, 0))
    return pl.pallas_call(
        _in_proj_kernel,
        grid=(n // tm,),
        in_specs=[tile, _resident((D_MODEL, N_IN_GROUPS * D_MODEL))],
        out_specs=[tile] * N_IN_GROUPS,
        out_shape=[jax.ShapeDtypeStruct((n, D_MODEL), F32)] * N_IN_GROUPS,
        compiler_params=_params(1),
        name="in_proj",
    )(xf, w_in_b)


def _cumsum_rows(g, row):
    b = g
    sh = 1
    while sh < g.shape[0]:
        b = b + jnp.where(row >= sh, pltpu.roll(b, sh, 0), 0.0)
        sh *= 2
    return b


def _hgrn_kernel(lbl_ref, gn_ref, q_ref, f_ref, v_ref, og_ref, s0_ref, y_ref, sout_ref, st_ref,
                 *, n_sub):
    t = pl.program_id(2)

    @pl.when(t == 0)
    def _init():
        st_ref[...] = s0_ref[...].T

    lg = lbl_ref[...]
    e = jnp.exp(lg - jnp.max(lg, axis=0, keepdims=True))
    lb = e[0:1, :] / jnp.sum(e, axis=0, keepdims=True)
    gn = gn_ref[...]

    row = lax.broadcasted_iota(jnp.int32, (HG_C, HG_DIM), 0)
    lane_rc = lax.broadcasted_iota(jnp.int32, (HG_R, HG_C), 1)
    trow_rc = lax.broadcasted_iota(jnp.int32, (HG_R, HG_C), 0)

    def chunk(c, carry):
        r = pl.multiple_of(c * HG_C, HG_C)
        q = q_ref[pl.ds(r, HG_C), :]
        v = v_ref[pl.ds(r, HG_C), :]
        f = lb + (1.0 - lb) * _sigmoid(f_ref[pl.ds(r, HG_C), :])
        g = jnp.log(f)
        kk = 1.0 - f
        b = _cumsum_rows(g, row)
        b_ex = jnp.where(row == 0, 0.0, pltpu.roll(b, 1, 0))
        st = st_ref[...]

        blocks = []
        for i in range(HG_C // HG_R):
            r0 = i * HG_R
            bi = b[r0:r0 + HG_R]
            qi = q[r0:r0 + HG_R]
            b0 = b_ex[r0:r0 + 1]
            qt = qi * jnp.exp(bi - b0)
            kt = kk * jnp.exp(jnp.minimum(b0 - b, 0.0))
            a_off = lax.dot_general(qt.astype(BF16), kt.astype(BF16), NT,
                                    preferred_element_type=F32)
            diag = jnp.zeros((HG_R, HG_C), F32)
            for sl in range(HG_R):
                s = r0 + sl
                p = qi * jnp.exp(jnp.minimum(bi - b[s:s + 1], 0.0)) * kk[s:s + 1]
                diag = jnp.where(lane_rc == s, jnp.sum(p, axis=1, keepdims=True), diag)
            blocks.append(jnp.where(lane_rc < r0, a_off,
                                    jnp.where(lane_rc <= r0 + trow_rc, diag, 0.0)))
        a = jnp.concatenate(blocks, axis=0)

        qh = q * jnp.exp(b)
        o = (jnp.dot(a.astype(BF16), v.astype(BF16), preferred_element_type=F32)
             + lax.dot_general(qh.astype(BF16), st.astype(BF16), NT, preferred_element_type=F32))
        bl = b[HG_C - 1:HG_C]
        kh = kk * jnp.exp(bl - b)
        st_ref[...] = st * jnp.exp(bl) + lax.dot_general(
            v.astype(BF16), kh.astype(BF16), TN, preferred_element_type=F32)

        ms = jnp.mean(o * o, axis=1, keepdims=True)
        y_ref[pl.ds(r, HG_C), :] = (o * lax.rsqrt(ms + LN_EPS) * gn
                                    * _sigmoid(og_ref[pl.ds(r, HG_C), :]))
        return carry

    lax.fori_loop(0, n_sub, chunk, 0)

    @pl.when(t == pl.num_programs(2) - 1)
    def _final():
        sout_ref[...] = st_ref[...].T


def _hgrn(lb_logits, gn, hq, hf, hi, hog, s0, tc):
    bsz, t_len, _ = hq.shape
    tok = pl.BlockSpec((None, tc, HG_DIM), lambda b, h, t: (b, t, h))
    state = pl.BlockSpec((None, None, HG_DIM, HG_DIM), lambda b, h, t: (b, h, 0, 0))
    return pl.pallas_call(
        functools.partial(_hgrn_kernel, n_sub=tc // HG_C),
        grid=(bsz, N_HEADS, t_len // tc),
        in_specs=[pl.BlockSpec((DEPTH + 1, HG_DIM), lambda b, h, t: (0, h)),
                  pl.BlockSpec((1, HG_DIM), lambda b, h, t: (0, 0)),
                  tok, tok, tok, tok, state],
        out_specs=[tok, state],
        out_shape=[jax.ShapeDtypeStruct(hq.shape, F32),
                   jax.ShapeDtypeStruct(s0.shape, F32)],
        scratch_shapes=[pltpu.VMEM((HG_DIM, HG_DIM), F32)],
        compiler_params=_params(3),
        name="hgrn",
    )(lb_logits, gn, hq, hf, hi, hog, s0)


def _lambda(lq1_ref, lk1_ref, lq2_ref, lk2_ref, lam_init):
    return (jnp.exp(jnp.sum(lq1_ref[...] * lk1_ref[...], axis=1, keepdims=True))
            - jnp.exp(jnp.sum(lq2_ref[...] * lk2_ref[...], axis=1, keepdims=True)) + lam_init)


def _split_maps(q):
    lane = lax.broadcasted_iota(jnp.int32, q.shape, 1)
    return jnp.where(lane < DA_HEAD_DIM, q, 0.0), jnp.where(lane >= DA_HEAD_DIM, q, 0.0)


def _attn_prompt_kernel(lq1_ref, lk1_ref, lq2_ref, lk2_ref, g_ref, q_ref, kn_ref, vn_ref, y_ref,
                        kb_ref, vt_ref, s_ref, *, n_q, lam_init):
    t = ATT_T
    kb_ref[...] = kn_ref[...].astype(BF16)
    vt_ref[...] = vn_ref[...].T.astype(BF16)

    lam = _lambda(lq1_ref, lk1_ref, lq2_ref, lk2_ref, lam_init)
    key_chunk = lax.broadcasted_iota(jnp.int32, (t, t), 0) // CHUNK
    qry_chunk = lax.broadcasted_iota(jnp.int32, (t, t), 1) // CHUNK
    diag_mask = key_chunk <= qry_chunk

    for qi in range(n_q):
        n_blk = qi + 1
        rows = slice(qi * t, (qi + 1) * t)
        qm = [m.astype(BF16) for m in _split_maps(q_ref[rows, :] * DA_SCALE)]
        outs = []
        for mp in range(2):
            m = jnp.full((1, t), NEG, F32)
            for c in range(n_blk):
                s = lax.dot_general(kb_ref[c * t:(c + 1) * t, :], qm[mp], NT,
                                    preferred_element_type=F32)
                if c == n_blk - 1:
                    s = jnp.where(diag_mask, s, NEG)
                s_ref[qi % 2, mp, c * t:(c + 1) * t, :] = s
                m = jnp.maximum(m, jnp.max(s, axis=0, keepdims=True))
            l = jnp.zeros((1, t), F32)
            acc = jnp.zeros((HG_DIM, t), F32)
            for c in range(n_blk):
                p = jnp.exp(s_ref[qi % 2, mp, c * t:(c + 1) * t, :] - m)
                l = l + jnp.sum(p, axis=0, keepdims=True)
                acc = acc + jnp.dot(vt_ref[:, c * t:(c + 1) * t], p.astype(BF16),
                                    preferred_element_type=F32)
            outs.append(acc / l)
        a = outs[0] - lam * outs[1]
        ms = jnp.mean(a * a, axis=0, keepdims=True)
        y = a * lax.rsqrt(ms + LN_EPS) * g_ref[...] * (1.0 - lam_init)
        y_ref[rows, :] = y.T


def _attn_prompt(lams, g_col, dq, dk, dv, lam_init):
    bsz, t_len, _ = dq.shape
    vec = pl.BlockSpec((1, DA_HEAD_DIM), lambda b, h: (0, 0))
    seq = pl.BlockSpec((None, t_len, HG_DIM), lambda b, h: (b, 0, h))
    return pl.pallas_call(
        functools.partial(_attn_prompt_kernel, n_q=t_len // ATT_T, lam_init=lam_init),
        grid=(bsz, N_HEADS),
        in_specs=[vec, vec, vec, vec, pl.BlockSpec((HG_DIM, 1), lambda b, h: (0, 0)),
                  seq, seq, seq],
        out_specs=seq,
        out_shape=jax.ShapeDtypeStruct(dq.shape, F32),
        scratch_shapes=[pltpu.VMEM((t_len, HG_DIM), BF16), pltpu.VMEM((HG_DIM, t_len), BF16),
                        pltpu.VMEM((2, 2, t_len, ATT_T), F32)],
        compiler_params=_params(2),
        name="attn_prompt",
    )(*lams, g_col, dq, dk, dv)


def _attn_sample_kernel(lq1_ref, lk1_ref, lq2_ref, lk2_ref, g_ref, q_ref, kn_ref, vn_ref,
                        kp_ref, vp_ref, y_ref, *, lam_init):
    tq = q_ref.shape[0]
    p_len = kp_ref.shape[0] // N_HEADS
    lam = _lambda(lq1_ref, lk1_ref, lq2_ref, lk2_ref, lam_init)
    for h in range(N_HEADS):
        cols = slice(h * HG_DIM, (h + 1) * HG_DIM)
        head_rows = pl.ds(h, p_len, stride=N_HEADS)
        qs = jnp.concatenate(_split_maps(q_ref[:, cols] * DA_SCALE), axis=0).astype(BF16)
        s_p = lax.dot_general(qs, kp_ref[head_rows, :].astype(BF16), NT,
                              preferred_element_type=F32)
        s_n = lax.dot_general(qs, kn_ref[:, cols].astype(BF16), NT, preferred_element_type=F32)
        m = jnp.maximum(jnp.max(s_p, axis=1, keepdims=True), jnp.max(s_n, axis=1, keepdims=True))
        p_p = jnp.exp(s_p - m)
        p_n = jnp.exp(s_n - m)
        l = jnp.sum(p_p, axis=1, keepdims=True) + jnp.sum(p_n, axis=1, keepdims=True)
        acc = (jnp.dot(p_p.astype(BF16), vp_ref[head_rows, :].astype(BF16),
                       preferred_element_type=F32)
               + jnp.dot(p_n.astype(BF16), vn_ref[:, cols].astype(BF16),
                         preferred_element_type=F32))
        o = acc / l
        a = o[:tq] - lam * o[tq:]
        ms = jnp.mean(a * a, axis=1, keepdims=True)
        y_ref[:, cols] = a * lax.rsqrt(ms + LN_EPS) * g_ref[...] * (1.0 - lam_init)


def _attn_sample(lams, g_row, dq, dk, dv, past_k, past_v, lam_init):
    bsz, t_len, _ = dq.shape
    p_rows = past_k.shape[1] * N_HEADS
    past_k = past_k.reshape(bsz, p_rows, HG_DIM)
    past_v = past_v.reshape(bsz, p_rows, HG_DIM)
    vec = pl.BlockSpec((1, DA_HEAD_DIM), lambda b: (0, 0))
    new = pl.BlockSpec((None, t_len, D_MODEL), lambda b: (b, 0, 0))
    past = pl.BlockSpec((None, p_rows, HG_DIM), lambda b: (b, 0, 0))
    return pl.pallas_call(
        functools.partial(_attn_sample_kernel, lam_init=lam_init),
        grid=(bsz,),
        in_specs=[vec, vec, vec, vec, pl.BlockSpec((1, HG_DIM), lambda b: (0, 0)),
                  new, new, new, past, past],
        out_specs=new,
        out_shape=jax.ShapeDtypeStruct(dq.shape, F32),
        compiler_params=_params(1),
        name="attn_sample",
    )(*lams, g_row, dq, dk, dv, past_k, past_v)


def _merge_kernel(x_ref, yh_ref, yd_ref, gh_ref, gd_ref, wh_ref, wd_ref, wo_ref, g_ref, b_ref, h_ref):
    mh = jnp.dot(yh_ref[...].astype(BF16), wh_ref[...], preferred_element_type=F32)
    md = jnp.dot(yd_ref[...].astype(BF16), wd_ref[...], preferred_element_type=F32)
    m = _sigmoid(gh_ref[...]) * mh + _sigmoid(gd_ref[...]) * md
    z = ALPHA * x_ref[...] + jnp.dot(m.astype(BF16), wo_ref[...], preferred_element_type=F32)
    h_ref[...] = _layer_norm(z, g_ref[...], b_ref[...])


def _merge(xf, yh, yd, gh, gd, wh, wd, wo, g, b, tm):
    n = xf.shape[0]
    tile = pl.BlockSpec((tm, D_MODEL), lambda i: (i, 0))
    wspec = _resident((D_MODEL, D_MODEL))
    vspec = _resident((1, D_MODEL))
    return pl.pallas_call(
        _merge_kernel,
        grid=(n // tm,),
        in_specs=[tile] * 5 + [wspec] * 3 + [vspec] * 2,
        out_specs=tile,
        out_shape=jax.ShapeDtypeStruct((n, D_MODEL), F32),
        compiler_params=_params(1),
        name="merge",
    )(xf, yh, yd, gh, gd, wh, wd, wo, g, b)


def _ffn_kernel(h_ref, buf0_ref, wup_ref, cw_ref, cb_ref, wdn_ref, g_ref, b_ref,
                out_ref, cst_ref, prev_ref, *, tm):
    t = pl.program_id(1)

    @pl.when(t == 0)
    def _init():
        prev_ref[...] = buf0_ref[...]

    h = h_ref[...]
    hb = h.astype(BF16)
    row = lax.broadcasted_iota(jnp.int32, (tm, FF_CHUNK), 0)

    def conv_cols(c0):
        cols = slice(c0, c0 + FF_CHUNK)
        up = jnp.dot(hb, wup_ref[:, cols], preferred_element_type=F32)
        pv = prev_ref[:, cols]
        up1 = jnp.where(row == 0, pv[1:2], pltpu.roll(up, 1, 0))
        up2 = jnp.where(row == 0, pv[0:1], jnp.where(row == 1, pv[1:2], pltpu.roll(up, 2, 0)))
        last = up[tm - (CONV_WIDTH - 1):tm]
        prev_ref[:, cols] = last
        cst_ref[:, cols] = last
        cw = cw_ref[:, cols]
        return cb_ref[:, cols] + cw[0:1] * up2 + cw[1:2] * up1 + cw[2:3] * up

    acc = jnp.zeros((tm, D_MODEL), F32)
    for c in range(D_FF // FF_CHUNK):
        val = conv_cols(c * FF_CHUNK)
        gate = conv_cols(D_FF + c * FF_CHUNK)
        act = gate * _sigmoid(gate) * val
        acc = acc + jnp.dot(act.astype(BF16), wdn_ref[c * FF_CHUNK:(c + 1) * FF_CHUNK, :],
                            preferred_element_type=F32)
    out_ref[...] = _layer_norm(ALPHA * h + acc, g_ref[...], b_ref[...])


def _ffn(h3, buf0, wup, cw, cb, wdn, g, b, tm):
    bsz, t_len, _ = h3.shape
    tile = pl.BlockSpec((None, tm, D_MODEL), lambda bi, t: (bi, t, 0))
    cst = pl.BlockSpec((None, CONV_WIDTH - 1, 2 * D_FF), lambda bi, t: (bi, 0, 0))
    return pl.pallas_call(
        functools.partial(_ffn_kernel, tm=tm),
        grid=(bsz, t_len // tm),
        in_specs=[tile, cst, _resident((D_MODEL, 2 * D_FF)), _resident((CONV_WIDTH, 2 * D_FF)),
                  _resident((1, 2 * D_FF)), _resident((D_FF, D_MODEL)),
                  _resident((1, D_MODEL)), _resident((1, D_MODEL))],
        out_specs=[tile, cst],
        out_shape=[jax.ShapeDtypeStruct(h3.shape, F32),
                   jax.ShapeDtypeStruct(buf0.shape, F32)],
        scratch_shapes=[pltpu.VMEM((CONV_WIDTH - 1, 2 * D_FF), F32)],
        compiler_params=_params(2),
        name="ffn",
    )(h3, buf0, wup, cw, cb, wdn, g, b)


def _pick_tile(n, want):
    tm = min(n, want)
    assert n % tm == 0, (n, tm)
    return tm


def _layer(x, layer_idx, w, s0, past_k, past_v, buf0):
    bsz, t_len, _ = x.shape
    n = bsz * t_len
    xf = x.reshape(n, D_MODEL)
    hq, hf, hi, hog, dq, dk, dv, g_hg, g_da = _in_proj(xf, w["w_in"], _pick_tile(n, 256))

    def seq(a):
        return a.reshape(bsz, t_len, D_MODEL)

    y_hg, s_new = _hgrn(w["lb_logits"], w["hg_norm_g"], seq(hq), seq(hf), seq(hi), seq(hog), s0,
                        _pick_tile(t_len, 512))
    lam_init = 0.8 - 0.6 * math.exp(-0.3 * layer_idx)
    if past_k is None:
        y_da = _attn_prompt(w["lams"], w["subln_g"].reshape(HG_DIM, 1), seq(dq), seq(dk), seq(dv),
                            lam_init)
    else:
        y_da = _attn_sample(w["lams"], w["subln_g"].reshape(1, HG_DIM), seq(dq), seq(dk), seq(dv),
                            past_k, past_v, lam_init)
    h = _merge(xf, y_hg.reshape(n, D_MODEL), y_da.reshape(n, D_MODEL), g_hg, g_da,
               w["w_br_hg"], w["w_br_da"], w["w_out"], w["ln1_g"], w["ln1_b"], _pick_tile(n, 256))
    out, cst = _ffn(seq(h), buf0, w["w_up"], w["conv_w"], w["conv_b"], w["w_down"],
                    w["ln2_g"], w["ln2_b"], _pick_tile(t_len, 256))
    new_k = dk.reshape(bsz, t_len, N_HEADS, 2 * DA_HEAD_DIM)
    new_v = dv.reshape(bsz, t_len, N_HEADS, 2 * DA_HEAD_DIM)
    return out, new_k, new_v, s_new, cst


def kernel(x_prompt, x_sample, cache_k, cache_v, state_hgrn, state_ffn_conv, w_in, hg_lb_logits,
           hg_norm_g, da_lambda_q1, da_lambda_k1, da_lambda_q2, da_lambda_k2, da_subln_g,
           w_br_hg, w_br_da, w_out, ln1_g, ln1_b, w_up, conv_w, conv_b, w_down, ln2_g, ln2_b):
    assert w_in.shape[0] == DEPTH == 1 and hg_lb_logits.shape[0] == DEPTH + 1
    l = 0
    w = dict(
        w_in=w_in[l].astype(BF16),
        lb_logits=hg_lb_logits.astype(F32),
        hg_norm_g=hg_norm_g[l].reshape(1, HG_DIM),
        lams=[a[l].reshape(1, DA_HEAD_DIM)
              for a in (da_lambda_q1, da_lambda_k1, da_lambda_q2, da_lambda_k2)],
        subln_g=da_subln_g[l],
        w_br_hg=w_br_hg[l].astype(BF16), w_br_da=w_br_da[l].astype(BF16),
        w_out=w_out[l].astype(BF16),
        ln1_g=ln1_g[l].reshape(1, D_MODEL), ln1_b=ln1_b[l].reshape(1, D_MODEL),
        w_up=w_up[l].astype(BF16), conv_w=conv_w[l], conv_b=conv_b[l].reshape(1, 2 * D_FF),
        w_down=w_down[l].astype(BF16),
        ln2_g=ln2_g[l].reshape(1, D_MODEL), ln2_b=ln2_b[l].reshape(1, D_MODEL),
    )
    bp = x_prompt.shape[0]
    s0_p = jnp.zeros((bp, N_HEADS, HG_DIM, HG_DIM), F32)
    buf0_p = jnp.zeros((bp, CONV_WIDTH - 1, 2 * D_FF), x_prompt.dtype)
    yp, kp, vp, sp, cp = _layer(x_prompt, l, w, s0_p, None, None, buf0_p)
    ys, ks, vs, ss, cs = _layer(x_sample, l, w, state_hgrn[l].astype(F32), cache_k[l], cache_v[l],
                                state_ffn_conv[l])
    return (yp, ys, kp[None], vp[None], sp[None], cp[None], ks[None], vs[None], ss[None], cs[None])
```

```python
import functools
import math

import jax
import jax.numpy as jnp
from jax import lax
from jax.experimental import pallas as pl
from jax.experimental.pallas import tpu as pltpu

F32 = jnp.float32
BF16 = jnp.bfloat16

D_MODEL = 1024
DEPTH = 1
CHUNK = 64
HG_DIM = 128
N_HEADS = D_MODEL // HG_DIM
DA_HEAD_DIM = 64
D_FF = 128 * ((8 * D_MODEL // 3 + 127) // 128)
CONV_WIDTH = 3
LN_EPS = 1e-5
ALPHA = (2 * DEPTH) ** 0.25
DA_SCALE = DA_HEAD_DIM ** -0.5
N_IN_GROUPS = 9
NEG = -1e30

VMEM_LIMIT = 56 * 1024 * 1024

HG_C = 64
HG_D = 8
HG_HPS = 8
FF_CHUNK = D_FF // 2
ATT_T = 256
VT_ROWS = HG_DIM + 16
LOG2E = math.log2(math.e)


def _params(n_axes):
    return pltpu.CompilerParams(dimension_semantics=("arbitrary",) * n_axes,
                                vmem_limit_bytes=VMEM_LIMIT)


def _resident(shape):
    nd = len(shape)
    return pl.BlockSpec(shape, lambda *_: (0,) * nd, pipeline_mode=pl.Buffered(1))


def _sigmoid(x):
    return 0.5 * jnp.tanh(0.5 * x) + 0.5


def _layer_norm(z, g, b):
    mu = jnp.mean(z, axis=-1, keepdims=True)
    zc = z - mu
    var = jnp.mean(zc * zc, axis=-1, keepdims=True)
    return zc * lax.rsqrt(var + LN_EPS) * g + b


NT = (((1,), (1,)), ((), ()))
TN = (((0,), (0,)), ((), ()))


def _in_proj_kernel(x_ref, w_ref, *out_refs):
    xb = x_ref[...].astype(BF16)
    for j, o_ref in enumerate(out_refs):
        o_ref[...] = jnp.dot(xb, w_ref[:, j * D_MODEL:(j + 1) * D_MODEL],
                             preferred_element_type=F32)


def _in_proj(xf, w_in_b, tm):
    n = xf.shape[0]
    tile = pl.BlockSpec((tm, D_MODEL), lambda i: (i, 0))
    return pl.pallas_call(
        _in_proj_kernel,
        grid=(n // tm,),
        in_specs=[tile, _resident((D_MODEL, N_IN_GROUPS * D_MODEL))],
        out_specs=[tile] * N_IN_GROUPS,
        out_shape=[jax.ShapeDtypeStruct((n, D_MODEL), F32)] * N_IN_GROUPS,
        compiler_params=_params(1),
        name="in_proj",
    )(xf, w_in_b)


def _cumsum_rows(g, row):
    b = g
    sh = 1
    while sh < g.shape[0]:
        b = b + jnp.where(row >= sh, pltpu.roll(b, sh, 0), 0.0)
        sh *= 2
    return b


def _block_rows(x, blk, offset):
    parts = [jnp.broadcast_to(x[j * blk + offset:j * blk + offset + 1], (blk, x.shape[1]))
             for j in range(x.shape[0] // blk)]
    return jnp.concatenate(parts, axis=0)


def _hgrn_chunk(q, hf, v, og, st, lb, gn, consts):
    row, level_masks, lane_dc, keep_dc = consts
    th = jnp.tanh(0.5 * hf)
    f = lb + (1.0 - lb) * (0.5 * th + 0.5)
    kk = (1.0 - lb) * (0.5 - 0.5 * th)
    b = _cumsum_rows(jnp.log2(f), row)
    b_ex = jnp.where(row == 0, 0.0, pltpu.roll(b, 1, 0))
    w = jnp.log2(kk) - b

    a = None
    for blk, mask in level_masks:
        qs = q * jnp.exp2(b - _block_rows(b_ex, blk, 0))
        ks = jnp.exp2(w + _block_rows(b, blk, blk - 1))
        a_l = lax.dot_general(qs.astype(BF16), ks.astype(BF16), NT, preferred_element_type=F32)
        a = jnp.where(mask, a_l, 0.0) if a is None else jnp.where(mask, a_l, a)
    diag = []
    for j in range(HG_C // HG_D):
        r0 = j * HG_D
        bj = b[r0:r0 + HG_D]
        qj = q[r0:r0 + HG_D]
        dj = jnp.zeros((HG_D, HG_C), F32)
        for sl in range(HG_D):
            s = r0 + sl
            p = qj * jnp.exp2(bj + w[s:s + 1])
            dj = jnp.where(lane_dc == s, jnp.sum(p, axis=1, keepdims=True), dj)
        diag.append(dj)
    a = jnp.where(keep_dc, jnp.concatenate(diag, axis=0), a)

    qh = q * jnp.exp2(b)
    o = (jnp.dot(a.astype(BF16), v.astype(BF16), preferred_element_type=F32)
         + lax.dot_general(qh.astype(BF16), st.astype(BF16), NT, preferred_element_type=F32))
    bl = b[HG_C - 1:HG_C]
    kh = jnp.exp2(w + bl)
    st_new = st * jnp.exp2(bl) + lax.dot_general(v.astype(BF16), kh.astype(BF16), TN,
                                                 preferred_element_type=F32)
    ms = jnp.mean(o * o, axis=1, keepdims=True)
    return o * lax.rsqrt(ms + LN_EPS) * gn * _sigmoid(og), st_new


def _hgrn_kernel(lbl_ref, gn_ref, q_ref, f_ref, v_ref, og_ref, s0_ref, y_ref, sout_ref, st_ref,
                 *, n_sub, hps):
    t = pl.program_id(2)

    @pl.when(t == 0)
    def _init():
        for hh in range(hps):
            st_ref[hh] = s0_ref[hh].T

    lg = lbl_ref[...]
    e = jnp.exp(lg - jnp.max(lg, axis=0, keepdims=True))
    lb_all = e[0:1, :] / jnp.sum(e, axis=0, keepdims=True)
    gn = gn_ref[...]

    row = lax.broadcasted_iota(jnp.int32, (HG_C, HG_DIM), 0)
    t_cc = lax.broadcasted_iota(jnp.int32, (HG_C, HG_C), 0)
    s_cc = lax.broadcasted_iota(jnp.int32, (HG_C, HG_C), 1)
    level_masks = []
    blk = HG_C // 2
    while blk >= HG_D:
        level_masks.append((blk, ((t_cc // blk) % 2 == 1) & (s_cc // blk == t_cc // blk - 1)))
        blk //= 2
    lane_dc = lax.broadcasted_iota(jnp.int32, (HG_D, HG_C), 1)
    keep_dc = (s_cc // HG_D == t_cc // HG_D) & (s_cc <= t_cc)
    consts = (row, level_masks, lane_dc, keep_dc)

    def chunk(c, carry):
        rows = pl.ds(pl.multiple_of(c * HG_C, HG_C), HG_C)
        for hh in range(hps):
            cols = slice(hh * HG_DIM, (hh + 1) * HG_DIM)
            y, st_new = _hgrn_chunk(q_ref[rows, cols], f_ref[rows, cols], v_ref[rows, cols],
                                    og_ref[rows, cols], st_ref[hh], lb_all[:, cols], gn, consts)
            st_ref[hh] = st_new
            y_ref[rows, cols] = y
        return carry

    lax.fori_loop(0, n_sub, chunk, 0)

    @pl.when(t == pl.num_programs(2) - 1)
    def _final():
        for hh in range(hps):
            sout_ref[hh] = st_ref[hh].T


def _hgrn(lb_logits, gn, hq, hf, hi, hog, s0, tc, hps):
    bsz, t_len, _ = hq.shape
    tok = pl.BlockSpec((None, tc, hps * HG_DIM), lambda b, h, t: (b, t, h))
    state = pl.BlockSpec((None, hps, HG_DIM, HG_DIM), lambda b, h, t: (b, h, 0, 0))
    return pl.pallas_call(
        functools.partial(_hgrn_kernel, n_sub=tc // HG_C, hps=hps),
        grid=(bsz, N_HEADS // hps, t_len // tc),
        in_specs=[pl.BlockSpec((DEPTH + 1, hps * HG_DIM), lambda b, h, t: (0, h)),
                  pl.BlockSpec((1, HG_DIM), lambda b, h, t: (0, 0)),
                  tok, tok, tok, tok, state],
        out_specs=[tok, state],
        out_shape=[jax.ShapeDtypeStruct(hq.shape, F32),
                   jax.ShapeDtypeStruct(s0.shape, F32)],
        scratch_shapes=[pltpu.VMEM((hps, HG_DIM, HG_DIM), F32)],
        compiler_params=_params(3),
        name="hgrn",
    )(lb_logits, gn, hq, hf, hi, hog, s0)


def _lambda(lq1_ref, lk1_ref, lq2_ref, lk2_ref, lam_init):
    return (jnp.exp(jnp.sum(lq1_ref[...] * lk1_ref[...], axis=1, keepdims=True))
            - jnp.exp(jnp.sum(lq2_ref[...] * lk2_ref[...], axis=1, keepdims=True)) + lam_init)


def _split_maps(q):
    lane = lax.broadcasted_iota(jnp.int32, q.shape, 1)
    return jnp.where(lane < DA_HEAD_DIM, q, 0.0), jnp.where(lane >= DA_HEAD_DIM, q, 0.0)


def _attn_prompt_kernel(lq1_ref, lk1_ref, lq2_ref, lk2_ref, g_ref, q_ref, kn_ref, vn_ref, y_ref,
                        kb_ref, vt_ref, *, n_q, lam_init):
    t = ATT_T
    t_len = kn_ref.shape[0]
    kb_ref[...] = kn_ref[...].astype(BF16)
    vt_ref[0:HG_DIM, :] = vn_ref[...].T.astype(BF16)
    pad_row = lax.broadcasted_iota(jnp.int32, (VT_ROWS - HG_DIM, t_len), 0)
    vt_ref[HG_DIM:VT_ROWS, :] = jnp.where(pad_row == 0, 1.0, 0.0).astype(BF16)

    lam = _lambda(lq1_ref, lk1_ref, lq2_ref, lk2_ref, lam_init)
    key_chunk = lax.broadcasted_iota(jnp.int32, (t, t), 0) // CHUNK
    qry_chunk = lax.broadcasted_iota(jnp.int32, (t, t), 1) // CHUNK
    diag_mask = key_chunk <= qry_chunk

    qm, m, acc, scores = {}, {}, {}, {}

    def score_pass(qi, c):
        if c == 0:
            q = q_ref[qi * t:(qi + 1) * t, :] * (DA_SCALE * LOG2E)
            qm[qi] = [x.astype(BF16) for x in _split_maps(q)]
            m[qi] = [jnp.full((1, t), NEG, F32)] * 2
        for mp in range(2):
            s = lax.dot_general(kb_ref[c * t:(c + 1) * t, :], qm[qi][mp], NT,
                                preferred_element_type=F32)
            if c == qi:
                s = jnp.where(diag_mask, s, NEG)
            scores[qi, mp, c] = s
            m[qi] = [jnp.maximum(m[qi][i], jnp.max(s, axis=0, keepdims=True)) if i == mp
                     else m[qi][i] for i in range(2)]

    def value_pass(qi, c):
        if c == 0:
            acc[qi] = [jnp.zeros((VT_ROWS, t), F32)] * 2
        for mp in range(2):
            p = jnp.exp2(scores.pop((qi, mp, c)) - m[qi][mp])
            pv = jnp.dot(vt_ref[:, c * t:(c + 1) * t], p.astype(BF16),
                         preferred_element_type=F32)
            acc[qi] = [acc[qi][i] + pv if i == mp else acc[qi][i] for i in range(2)]
        if c == qi:
            o = [x[0:HG_DIM] / x[HG_DIM:HG_DIM + 1] for x in acc[qi]]
            a = o[0] - lam * o[1]
            ms = jnp.mean(a * a, axis=0, keepdims=True)
            y = a * lax.rsqrt(ms + LN_EPS) * g_ref[...] * (1.0 - lam_init)
            y_ref[qi * t:(qi + 1) * t, :] = y.T

    for qi in range(n_q + 1):
        for c in range(qi + 1):
            if qi < n_q:
                score_pass(qi, c)
            if qi >= 1 and c < qi:
                value_pass(qi - 1, c)


def _attn_prompt(lams, g_col, dq, dk, dv, lam_init):
    bsz, t_len, _ = dq.shape
    vec = pl.BlockSpec((1, DA_HEAD_DIM), lambda b, h: (0, 0))
    seq = pl.BlockSpec((None, t_len, HG_DIM), lambda b, h: (b, 0, h))
    return pl.pallas_call(
        functools.partial(_attn_prompt_kernel, n_q=t_len // ATT_T, lam_init=lam_init),
        grid=(bsz, N_HEADS),
        in_specs=[vec, vec, vec, vec, pl.BlockSpec((HG_DIM, 1), lambda b, h: (0, 0)),
                  seq, seq, seq],
        out_specs=seq,
        out_shape=jax.ShapeDtypeStruct(dq.shape, F32),
        scratch_shapes=[pltpu.VMEM((t_len, HG_DIM), BF16), pltpu.VMEM((VT_ROWS, t_len), BF16)],
        compiler_params=_params(2),
        name="attn_prompt",
    )(*lams, g_col, dq, dk, dv)


def _attn_sample_kernel(lq1_ref, lk1_ref, lq2_ref, lk2_ref, g_ref, q_ref, kn_ref, vn_ref,
                        kp_ref, vp_ref, y_ref, *, lam_init):
    tq = q_ref.shape[0]
    p_len = kp_ref.shape[0] // N_HEADS
    lam = _lambda(lq1_ref, lk1_ref, lq2_ref, lk2_ref, lam_init)
    for h in range(N_HEADS):
        cols = slice(h * HG_DIM, (h + 1) * HG_DIM)
        head_rows = pl.ds(h, p_len, stride=N_HEADS)
        qs = jnp.concatenate(_split_maps(q_ref[:, cols] * (DA_SCALE * LOG2E)),
                             axis=0).astype(BF16)
        s_p = lax.dot_general(qs, kp_ref[head_rows, :].astype(BF16), NT,
                              preferred_element_type=F32)
        s_n = lax.dot_general(qs, kn_ref[:, cols].astype(BF16), NT, preferred_element_type=F32)
        m = jnp.maximum(jnp.max(s_p, axis=1, keepdims=True), jnp.max(s_n, axis=1, keepdims=True))
        p_p = jnp.exp2(s_p - m)
        p_n = jnp.exp2(s_n - m)
        l = jnp.sum(p_p, axis=1, keepdims=True) + jnp.sum(p_n, axis=1, keepdims=True)
        acc = (jnp.dot(p_p.astype(BF16), vp_ref[head_rows, :].astype(BF16),
                       preferred_element_type=F32)
               + jnp.dot(p_n.astype(BF16), vn_ref[:, cols].astype(BF16),
                         preferred_element_type=F32))
        o = acc / l
        a = o[:tq] - lam * o[tq:]
        ms = jnp.mean(a * a, axis=1, keepdims=True)
        y_ref[:, cols] = a * lax.rsqrt(ms + LN_EPS) * g_ref[...] * (1.0 - lam_init)


def _attn_sample(lams, g_row, dq, dk, dv, past_k, past_v, lam_init):
    bsz, t_len, _ = dq.shape
    p_rows = past_k.shape[1] * N_HEADS
    past_k = past_k.reshape(bsz, p_rows, HG_DIM)
    past_v = past_v.reshape(bsz, p_rows, HG_DIM)
    vec = pl.BlockSpec((1, DA_HEAD_DIM), lambda b: (0, 0))
    new = pl.BlockSpec((None, t_len, D_MODEL), lambda b: (b, 0, 0))
    past = pl.BlockSpec((None, p_rows, HG_DIM), lambda b: (b, 0, 0))
    return pl.pallas_call(
        functools.partial(_attn_sample_kernel, lam_init=lam_init),
        grid=(bsz,),
        in_specs=[vec, vec, vec, vec, pl.BlockSpec((1, HG_DIM), lambda b: (0, 0)),
                  new, new, new, past, past],
        out_specs=new,
        out_shape=jax.ShapeDtypeStruct(dq.shape, F32),
        compiler_params=_params(1),
        name="attn_sample",
    )(*lams, g_row, dq, dk, dv, past_k, past_v)


def _merge_kernel(x_ref, yh_ref, yd_ref, gh_ref, gd_ref, wh_ref, wd_ref, wo_ref, g_ref, b_ref, h_ref):
    mh = jnp.dot(yh_ref[...].astype(BF16), wh_ref[...], preferred_element_type=F32)
    md = jnp.dot(yd_ref[...].astype(BF16), wd_ref[...], preferred_element_type=F32)
    m = _sigmoid(gh_ref[...]) * mh + _sigmoid(gd_ref[...]) * md
    z = ALPHA * x_ref[...] + jnp.dot(m.astype(BF16), wo_ref[...], preferred_element_type=F32)
    h_ref[...] = _layer_norm(z, g_ref[...], b_ref[...])


def _merge(xf, yh, yd, gh, gd, wh, wd, wo, g, b, tm):
    n = xf.shape[0]
    tile = pl.BlockSpec((tm, D_MODEL), lambda i: (i, 0))
    wspec = _resident((D_MODEL, D_MODEL))
    vspec = _resident((1, D_MODEL))
    return pl.pallas_call(
        _merge_kernel,
        grid=(n // tm,),
        in_specs=[tile] * 5 + [wspec] * 3 + [vspec] * 2,
        out_specs=tile,
        out_shape=jax.ShapeDtypeStruct((n, D_MODEL), F32),
        compiler_params=_params(1),
        name="merge",
    )(xf, yh, yd, gh, gd, wh, wd, wo, g, b)


def _ffn_kernel(h_ref, buf0_ref, wup_ref, cw_ref, cb_ref, wdn_ref, g_ref, b_ref,
                out_ref, cst_ref, prev_ref, *, tm):
    t = pl.program_id(1)

    @pl.when(t == 0)
    def _init():
        prev_ref[...] = buf0_ref[...]

    h = h_ref[...]
    hb = h.astype(BF16)
    row = lax.broadcasted_iota(jnp.int32, (tm, FF_CHUNK), 0)

    def conv_cols(c0):
        cols = slice(c0, c0 + FF_CHUNK)
        up = jnp.dot(hb, wup_ref[:, cols], preferred_element_type=F32)
        pv = prev_ref[:, cols]
        up1 = jnp.where(row == 0, pv[1:2], pltpu.roll(up, 1, 0))
        up2 = jnp.where(row == 0, pv[0:1], jnp.where(row == 1, pv[1:2], pltpu.roll(up, 2, 0)))
        last = up[tm - (CONV_WIDTH - 1):tm]
        prev_ref[:, cols] = last
        cst_ref[:, cols] = last
        cw = cw_ref[:, cols]
        return cb_ref[:, cols] + cw[0:1] * up2 + cw[1:2] * up1 + cw[2:3] * up

    acc = jnp.zeros((tm, D_MODEL), F32)
    for c in range(D_FF // FF_CHUNK):
        val = conv_cols(c * FF_CHUNK)
        gate = conv_cols(D_FF + c * FF_CHUNK)
        act = gate * _sigmoid(gate) * val
        acc = acc + jnp.dot(act.astype(BF16), wdn_ref[c * FF_CHUNK:(c + 1) * FF_CHUNK, :],
                            preferred_element_type=F32)
    out_ref[...] = _layer_norm(ALPHA * h + acc, g_ref[...], b_ref[...])


def _ffn(h3, buf0, wup, cw, cb, wdn, g, b, tm):
    bsz, t_len, _ = h3.shape
    tile = pl.BlockSpec((None, tm, D_MODEL), lambda bi, t: (bi, t, 0))
    cst = pl.BlockSpec((None, CONV_WIDTH - 1, 2 * D_FF), lambda bi, t: (bi, 0, 0))
    return pl.pallas_call(
        functools.partial(_ffn_kernel, tm=tm),
        grid=(bsz, t_len // tm),
        in_specs=[tile, cst, _resident((D_MODEL, 2 * D_FF)), _resident((CONV_WIDTH, 2 * D_FF)),
                  _resident((1, 2 * D_FF)), _resident((D_FF, D_MODEL)),
                  _resident((1, D_MODEL)), _resident((1, D_MODEL))],
        out_specs=[tile, cst],
        out_shape=[jax.ShapeDtypeStruct(h3.shape, F32),
                   jax.ShapeDtypeStruct(buf0.shape, F32)],
        scratch_shapes=[pltpu.VMEM((CONV_WIDTH - 1, 2 * D_FF), F32)],
        compiler_params=_params(2),
        name="ffn",
    )(h3, buf0, wup, cw, cb, wdn, g, b)


def _pick_tile(n, want):
    tm = min(n, want)
    assert n % tm == 0, (n, tm)
    return tm


def _layer(x, layer_idx, w, s0, past_k, past_v, buf0):
    bsz, t_len, _ = x.shape
    n = bsz * t_len
    xf = x.reshape(n, D_MODEL)
    hq, hf, hi, hog, dq, dk, dv, g_hg, g_da = _in_proj(xf, w["w_in"], _pick_tile(n, 256))

    def seq(a):
        return a.reshape(bsz, t_len, D_MODEL)

    y_hg, s_new = _hgrn(w["lb_logits"], w["hg_norm_g"], seq(hq), seq(hf), seq(hi), seq(hog), s0,
                        _pick_tile(t_len, 512), HG_HPS)
    lam_init = 0.8 - 0.6 * math.exp(-0.3 * layer_idx)
    if past_k is None:
        y_da = _attn_prompt(w["lams"], w["subln_g"].reshape(HG_DIM, 1), seq(dq), seq(dk), seq(dv),
                            lam_init)
    else:
        y_da = _attn_sample(w["lams"], w["subln_g"].reshape(1, HG_DIM), seq(dq), seq(dk), seq(dv),
                            past_k, past_v, lam_init)
    h = _merge(xf, y_hg.reshape(n, D_MODEL), y_da.reshape(n, D_MODEL), g_hg, g_da,
               w["w_br_hg"], w["w_br_da"], w["w_out"], w["ln1_g"], w["ln1_b"], _pick_tile(n, 256))
    out, cst = _ffn(seq(h), buf0, w["w_up"], w["conv_w"], w["conv_b"], w["w_down"],
                    w["ln2_g"], w["ln2_b"], _pick_tile(t_len, 256))
    new_k = dk.reshape(bsz, t_len, N_HEADS, 2 * DA_HEAD_DIM)
    new_v = dv.reshape(bsz, t_len, N_HEADS, 2 * DA_HEAD_DIM)
    return out, new_k, new_v, s_new, cst


def kernel(x_prompt, x_sample, cache_k, cache_v, state_hgrn, state_ffn_conv, w_in, hg_lb_logits,
           hg_norm_g, da_lambda_q1, da_lambda_k1, da_lambda_q2, da_lambda_k2, da_subln_g,
           w_br_hg, w_br_da, w_out, ln1_g, ln1_b, w_up, conv_w, conv_b, w_down, ln2_g, ln2_b):
    assert w_in.shape[0] == DEPTH == 1 and hg_lb_logits.shape[0] == DEPTH + 1
    l = 0
    w = dict(
        w_in=w_in[l].astype(BF16),
        lb_logits=hg_lb_logits.astype(F32),
        hg_norm_g=hg_norm_g[l].reshape(1, HG_DIM),
        lams=[a[l].reshape(1, DA_HEAD_DIM)
              for a in (da_lambda_q1, da_lambda_k1, da_lambda_q2, da_lambda_k2)],
        subln_g=da_subln_g[l],
        w_br_hg=w_br_hg[l].astype(BF16), w_br_da=w_br_da[l].astype(BF16),
        w_out=w_out[l].astype(BF16),
        ln1_g=ln1_g[l].reshape(1, D_MODEL), ln1_b=ln1_b[l].reshape(1, D_MODEL),
        w_up=w_up[l].astype(BF16), conv_w=conv_w[l], conv_b=conv_b[l].reshape(1, 2 * D_FF),
        w_down=w_down[l].astype(BF16),
        ln2_g=ln2_g[l].reshape(1, D_MODEL), ln2_b=ln2_b[l].reshape(1, D_MODEL),
    )
    bp = x_prompt.shape[0]
    s0_p = jnp.zeros((bp, N_HEADS, HG_DIM, HG_DIM), F32)
    buf0_p = jnp.zeros((bp, CONV_WIDTH - 1, 2 * D_FF), x_prompt.dtype)
    yp, kp, vp, sp, cp = _layer(x_prompt, l, w, s0_p, None, None, buf0_p)
    ys, ks, vs, ss, cs = _layer(x_sample, l, w, state_hgrn[l].astype(F32), cache_k[l], cache_v[l],
                                state_ffn_conv[l])
    return (yp, ys, kp[None], vp[None], sp[None], cp[None], ks[None], vs[None], ss[None], cs[None])
```

```python
import functools
import math

import jax
import jax.numpy as jnp
from jax import lax
from jax.experimental import pallas as pl
from jax.experimental.pallas import tpu as pltpu

F32 = jnp.float32
BF16 = jnp.bfloat16

D_MODEL = 1024
DEPTH = 1
CHUNK = 64
HG_DIM = 128
N_HEADS = D_MODEL // HG_DIM
DA_HEAD_DIM = 64
D_FF = 128 * ((8 * D_MODEL // 3 + 127) // 128)
CONV_WIDTH = 3
LN_EPS = 1e-5
ALPHA = (2 * DEPTH) ** 0.25
DA_SCALE = DA_HEAD_DIM ** -0.5
N_MIX_GROUPS = 6
N_GATE_GROUPS = 3
NEG = -1e30

VMEM_LIMIT = 56 * 1024 * 1024
SUBLANES = 8
ROW_TILE = 256

HG_C = 64
HG_D = 8
HG_HPS = 8
FF_CHUNK = D_FF // 2
ATT_T = 256
VT_ROWS = HG_DIM + 16
LOG2E = math.log2(math.e)


def _params(n_axes):
    return pltpu.CompilerParams(dimension_semantics=("arbitrary",) * n_axes,
                                vmem_limit_bytes=VMEM_LIMIT)


def _resident(shape):
    nd = len(shape)
    return pl.BlockSpec(shape, lambda *_: (0,) * nd, pipeline_mode=pl.Buffered(1))


def _sigmoid(x):
    return 0.5 * jnp.tanh(0.5 * x) + 0.5


def _layer_norm(z, g, b):
    mu = jnp.mean(z, axis=-1, keepdims=True)
    zc = z - mu
    var = jnp.mean(zc * zc, axis=-1, keepdims=True)
    return zc * lax.rsqrt(var + LN_EPS) * g + b


NT = (((1,), (1,)), ((), ()))
TN = (((0,), (0,)), ((), ()))


def _two_group_tiles(tm, n_p, width=D_MODEL):
    prompt = pl.BlockSpec((tm, width), lambda i: (jnp.minimum(i, n_p - 1), 0))
    sample = pl.BlockSpec((tm, width), lambda i: (jnp.maximum(i - n_p, 0), 0))
    return prompt, sample


def _in_proj_kernel(xp_ref, xs_ref, w_ref, *out_refs, n_p):
    i = pl.program_id(0)

    def run(x_ref, outs):
        xb = x_ref[...].astype(BF16)
        for j, o_ref in enumerate(outs):
            o_ref[...] = jnp.dot(xb, w_ref[:, j * D_MODEL:(j + 1) * D_MODEL],
                                 preferred_element_type=F32)

    pl.when(i < n_p)(functools.partial(run, xp_ref, out_refs[:N_MIX_GROUPS]))
    pl.when(i >= n_p)(functools.partial(run, xs_ref, out_refs[N_MIX_GROUPS:]))


def _in_proj(xp, xs, w_mix, tm):
    n_p, n_s = xp.shape[0] // tm, xs.shape[0] // tm
    tile_p, tile_s = _two_group_tiles(tm, n_p)
    outs = pl.pallas_call(
        functools.partial(_in_proj_kernel, n_p=n_p),
        grid=(n_p + n_s,),
        in_specs=[tile_p, tile_s, _resident((D_MODEL, N_MIX_GROUPS * D_MODEL))],
        out_specs=[tile_p] * N_MIX_GROUPS + [tile_s] * N_MIX_GROUPS,
        out_shape=([jax.ShapeDtypeStruct(xp.shape, F32)] * N_MIX_GROUPS
                   + [jax.ShapeDtypeStruct(xs.shape, F32)] * N_MIX_GROUPS),
        compiler_params=_params(1),
        name="in_proj",
    )(xp, xs, w_mix)
    return outs[:N_MIX_GROUPS], outs[N_MIX_GROUPS:]


def _cumsum_rows(g, row):
    b = g
    sh = 1
    while sh < g.shape[0]:
        b = b + jnp.where(row >= sh, pltpu.roll(b, sh, 0), 0.0)
        sh *= 2
    return b


def _block_rows(x, blk, offset):
    parts = [jnp.broadcast_to(x[j * blk + offset:j * blk + offset + 1], (blk, x.shape[1]))
             for j in range(x.shape[0] // blk)]
    return jnp.concatenate(parts, axis=0)


def _hgrn_chunk(q, hf, v, st, lb, gn, consts):
    row, level_masks, lane_dc, keep_dc = consts
    th = jnp.tanh(0.5 * hf)
    f = lb + (1.0 - lb) * (0.5 * th + 0.5)
    kk = (1.0 - lb) * (0.5 - 0.5 * th)
    b = _cumsum_rows(jnp.log2(f), row)
    b_ex = jnp.where(row == 0, 0.0, pltpu.roll(b, 1, 0))
    w = jnp.log2(kk) - b

    a = None
    for blk, mask in level_masks:
        qs = q * jnp.exp2(b - _block_rows(b_ex, blk, 0))
        ks = jnp.exp2(w + _block_rows(b, blk, blk - 1))
        a_l = lax.dot_general(qs.astype(BF16), ks.astype(BF16), NT, preferred_element_type=F32)
        a = jnp.where(mask, a_l, 0.0) if a is None else jnp.where(mask, a_l, a)
    diag = []
    for j in range(HG_C // HG_D):
        r0 = j * HG_D
        bj = b[r0:r0 + HG_D]
        qj = q[r0:r0 + HG_D]
        dj = jnp.zeros((HG_D, HG_C), F32)
        for sl in range(HG_D):
            s = r0 + sl
            p = qj * jnp.exp2(bj + w[s:s + 1])
            dj = jnp.where(lane_dc == s, jnp.sum(p, axis=1, keepdims=True), dj)
        diag.append(dj)
    a = jnp.where(keep_dc, jnp.concatenate(diag, axis=0), a)

    qh = q * jnp.exp2(b)
    o = (jnp.dot(a.astype(BF16), v.astype(BF16), preferred_element_type=F32)
         + lax.dot_general(qh.astype(BF16), st.astype(BF16), NT, preferred_element_type=F32))
    bl = b[HG_C - 1:HG_C]
    kh = jnp.exp2(w + bl)
    st_new = st * jnp.exp2(bl) + lax.dot_general(v.astype(BF16), kh.astype(BF16), TN,
                                                 preferred_element_type=F32)
    ms = jnp.mean(o * o, axis=1, keepdims=True)
    return o * lax.rsqrt(ms + LN_EPS) * gn, st_new


def _hgrn_kernel(lbl_ref, gn_ref, q_ref, f_ref, v_ref, s0_ref, y_ref, sout_ref, st_ref,
                 *, n_sub, hps):
    t = pl.program_id(2)

    @pl.when(t == 0)
    def _init():
        for hh in range(hps):
            st_ref[hh] = s0_ref[hh].T

    lg = lbl_ref[...]
    e = jnp.exp(lg - jnp.max(lg, axis=0, keepdims=True))
    lb_all = e[0:1, :] / jnp.sum(e, axis=0, keepdims=True)
    gn = gn_ref[...]

    row = lax.broadcasted_iota(jnp.int32, (HG_C, HG_DIM), 0)
    t_cc = lax.broadcasted_iota(jnp.int32, (HG_C, HG_C), 0)
    s_cc = lax.broadcasted_iota(jnp.int32, (HG_C, HG_C), 1)
    level_masks = []
    blk = HG_C // 2
    while blk >= HG_D:
        level_masks.append((blk, ((t_cc // blk) % 2 == 1) & (s_cc // blk == t_cc // blk - 1)))
        blk //= 2
    lane_dc = lax.broadcasted_iota(jnp.int32, (HG_D, HG_C), 1)
    keep_dc = (s_cc // HG_D == t_cc // HG_D) & (s_cc <= t_cc)
    consts = (row, level_masks, lane_dc, keep_dc)

    def chunk(c, carry):
        rows = pl.ds(pl.multiple_of(c * HG_C, HG_C), HG_C)
        for hh in range(hps):
            cols = slice(hh * HG_DIM, (hh + 1) * HG_DIM)
            y, st_new = _hgrn_chunk(q_ref[rows, cols], f_ref[rows, cols], v_ref[rows, cols],
                                    st_ref[hh], lb_all[:, cols], gn, consts)
            st_ref[hh] = st_new
            y_ref[rows, cols] = y
        return carry

    lax.fori_loop(0, n_sub, chunk, 0)

    @pl.when(t == pl.num_programs(2) - 1)
    def _final():
        for hh in range(hps):
            sout_ref[hh] = st_ref[hh].T


def _hgrn(lb_logits, gn, hq, hf, hi, s0, tc, hps):
    bsz, t_len, _ = hq.shape
    assert t_len % tc == 0 and tc % HG_C == 0 and N_HEADS % hps == 0
    tok = pl.BlockSpec((None, tc, hps * HG_DIM), lambda b, h, t: (b, t, h))
    state = pl.BlockSpec((None, hps, HG_DIM, HG_DIM), lambda b, h, t: (b, h, 0, 0))
    return pl.pallas_call(
        functools.partial(_hgrn_kernel, n_sub=tc // HG_C, hps=hps),
        grid=(bsz, N_HEADS // hps, t_len // tc),
        in_specs=[pl.BlockSpec((DEPTH + 1, hps * HG_DIM), lambda b, h, t: (0, h)),
                  pl.BlockSpec((1, HG_DIM), lambda b, h, t: (0, 0)),
                  tok, tok, tok, state],
        out_specs=[tok, state],
        out_shape=[jax.ShapeDtypeStruct(hq.shape, F32),
                   jax.ShapeDtypeStruct(s0.shape, F32)],
        scratch_shapes=[pltpu.VMEM((hps, HG_DIM, HG_DIM), F32)],
        compiler_params=_params(3),
        name="hgrn",
    )(lb_logits, gn, hq, hf, hi, s0)


def _lambda(lq1_ref, lk1_ref, lq2_ref, lk2_ref, lam_init):
    return (jnp.exp(jnp.sum(lq1_ref[...] * lk1_ref[...], axis=1, keepdims=True))
            - jnp.exp(jnp.sum(lq2_ref[...] * lk2_ref[...], axis=1, keepdims=True)) + lam_init)


def _split_maps(q):
    lane = lax.broadcasted_iota(jnp.int32, q.shape, 1)
    return jnp.where(lane < DA_HEAD_DIM, q, 0.0), jnp.where(lane >= DA_HEAD_DIM, q, 0.0)


def _attn_prompt_kernel(lq1_ref, lk1_ref, lq2_ref, lk2_ref, g_ref, q_ref, kn_ref, vn_ref, y_ref,
                        kb_ref, vt_ref, *, n_q, lam_init):
    t = ATT_T
    t_len = kn_ref.shape[0]
    kb_ref[...] = kn_ref[...].astype(BF16)
    vt_ref[0:HG_DIM, :] = vn_ref[...].T.astype(BF16)
    pad_row = lax.broadcasted_iota(jnp.int32, (VT_ROWS - HG_DIM, t_len), 0)
    vt_ref[HG_DIM:VT_ROWS, :] = jnp.where(pad_row == 0, 1.0, 0.0).astype(BF16)

    lam = _lambda(lq1_ref, lk1_ref, lq2_ref, lk2_ref, lam_init)
    key_chunk = lax.broadcasted_iota(jnp.int32, (t, t), 0) // CHUNK
    qry_chunk = lax.broadcasted_iota(jnp.int32, (t, t), 1) // CHUNK
    diag_mask = key_chunk <= qry_chunk

    qm, m, acc, scores = {}, {}, {}, {}

    def score_pass(qi, c):
        if c == 0:
            q = q_ref[qi * t:(qi + 1) * t, :] * (DA_SCALE * LOG2E)
            qm[qi] = [x.astype(BF16) for x in _split_maps(q)]
            m[qi] = [jnp.full((1, t), NEG, F32)] * 2
        for mp in range(2):
            s = lax.dot_general(kb_ref[c * t:(c + 1) * t, :], qm[qi][mp], NT,
                                preferred_element_type=F32)
            if c == qi:
                s = jnp.where(diag_mask, s, NEG)
            scores[qi, mp, c] = s
            m[qi] = [jnp.maximum(m[qi][i], jnp.max(s, axis=0, keepdims=True)) if i == mp
                     else m[qi][i] for i in range(2)]

    def value_pass(qi, c):
        if c == 0:
            acc[qi] = [jnp.zeros((VT_ROWS, t), F32)] * 2
        for mp in range(2):
            p = jnp.exp2(scores.pop((qi, mp, c)) - m[qi][mp])
            pv = jnp.dot(vt_ref[:, c * t:(c + 1) * t], p.astype(BF16),
                         preferred_element_type=F32)
            acc[qi] = [acc[qi][i] + pv if i == mp else acc[qi][i] for i in range(2)]
        if c == qi:
            o = [x[0:HG_DIM] / x[HG_DIM:HG_DIM + 1] for x in acc[qi]]
            a = o[0] - lam * o[1]
            ms = jnp.mean(a * a, axis=0, keepdims=True)
            y = a * lax.rsqrt(ms + LN_EPS) * g_ref[...] * (1.0 - lam_init)
            y_ref[qi * t:(qi + 1) * t, :] = y.T

    for qi in range(n_q + 1):
        for c in range(qi + 1):
            if qi < n_q:
                score_pass(qi, c)
            if qi >= 1 and c < qi:
                value_pass(qi - 1, c)


def _attn_prompt(lams, g_col, dq, dk, dv, lam_init):
    bsz, t_len, _ = dq.shape
    vec = pl.BlockSpec((1, DA_HEAD_DIM), lambda b, h: (0, 0))
    seq = pl.BlockSpec((None, t_len, HG_DIM), lambda b, h: (b, 0, h))
    return pl.pallas_call(
        functools.partial(_attn_prompt_kernel, n_q=t_len // ATT_T, lam_init=lam_init),
        grid=(bsz, N_HEADS),
        in_specs=[vec, vec, vec, vec, pl.BlockSpec((HG_DIM, 1), lambda b, h: (0, 0)),
                  seq, seq, seq],
        out_specs=seq,
        out_shape=jax.ShapeDtypeStruct(dq.shape, F32),
        scratch_shapes=[pltpu.VMEM((t_len, HG_DIM), BF16), pltpu.VMEM((VT_ROWS, t_len), BF16)],
        compiler_params=_params(2),
        name="attn_prompt",
    )(*lams, g_col, dq, dk, dv)


def _attn_sample_kernel(lq1_ref, lk1_ref, lq2_ref, lk2_ref, g_ref, q_ref, kn_ref, vn_ref,
                        kp_ref, vp_ref, y_ref, *, lam_init):
    tq = q_ref.shape[0]
    p_len = kp_ref.shape[0] // N_HEADS
    lam = _lambda(lq1_ref, lk1_ref, lq2_ref, lk2_ref, lam_init)
    for h in range(N_HEADS):
        cols = slice(h * HG_DIM, (h + 1) * HG_DIM)
        head_rows = pl.ds(h, p_len, stride=N_HEADS)
        qs = jnp.concatenate(_split_maps(q_ref[:, cols] * (DA_SCALE * LOG2E)),
                             axis=0).astype(BF16)
        s_p = lax.dot_general(qs, kp_ref[head_rows, :].astype(BF16), NT,
                              preferred_element_type=F32)
        s_n = lax.dot_general(qs, kn_ref[:, cols].astype(BF16), NT, preferred_element_type=F32)
        m = jnp.maximum(jnp.max(s_p, axis=1, keepdims=True), jnp.max(s_n, axis=1, keepdims=True))
        p_p = jnp.exp2(s_p - m)
        p_n = jnp.exp2(s_n - m)
        l = jnp.sum(p_p, axis=1, keepdims=True) + jnp.sum(p_n, axis=1, keepdims=True)
        acc = (jnp.dot(p_p.astype(BF16), vp_ref[head_rows, :].astype(BF16),
                       preferred_element_type=F32)
               + jnp.dot(p_n.astype(BF16), vn_ref[:, cols].astype(BF16),
                         preferred_element_type=F32))
        o = acc / l
        a = o[:tq] - lam * o[tq:]
        ms = jnp.mean(a * a, axis=1, keepdims=True)
        y_ref[:, cols] = a * lax.rsqrt(ms + LN_EPS) * g_ref[...] * (1.0 - lam_init)


def _attn_sample(lams, g_row, dq, dk, dv, past_k, past_v, lam_init):
    bsz, t_len, _ = dq.shape
    p_rows = past_k.shape[1] * N_HEADS
    past_k = past_k.reshape(bsz, p_rows, HG_DIM)
    past_v = past_v.reshape(bsz, p_rows, HG_DIM)
    vec = pl.BlockSpec((1, DA_HEAD_DIM), lambda b: (0, 0))
    new = pl.BlockSpec((None, t_len, D_MODEL), lambda b: (b, 0, 0))
    past = pl.BlockSpec((None, p_rows, HG_DIM), lambda b: (b, 0, 0))
    return pl.pallas_call(
        functools.partial(_attn_sample_kernel, lam_init=lam_init),
        grid=(bsz,),
        in_specs=[vec, vec, vec, vec, pl.BlockSpec((1, HG_DIM), lambda b: (0, 0)),
                  new, new, new, past, past],
        out_specs=new,
        out_shape=jax.ShapeDtypeStruct(dq.shape, F32),
        compiler_params=_params(1),
        name="attn_sample",
    )(*lams, g_row, dq, dk, dv, past_k, past_v)


def _merge_kernel(xp_ref, xs_ref, yhp_ref, yhs_ref, ydp_ref, yds_ref, wg_ref, wh_ref, wd_ref, wo_ref,
                  g_ref, b_ref, hp_ref, hs_ref, *, n_p):
    i = pl.program_id(0)

    def run(x_ref, yh_ref, yd_ref, h_ref):
        x = x_ref[...]
        gates = jnp.dot(x.astype(BF16), wg_ref[...], preferred_element_type=F32)
        y_hg = yh_ref[...] * _sigmoid(gates[:, 0:D_MODEL])
        mh = jnp.dot(y_hg.astype(BF16), wh_ref[...], preferred_element_type=F32)
        md = jnp.dot(yd_ref[...].astype(BF16), wd_ref[...], preferred_element_type=F32)
        m = (_sigmoid(gates[:, D_MODEL:2 * D_MODEL]) * mh
             + _sigmoid(gates[:, 2 * D_MODEL:3 * D_MODEL]) * md)
        z = ALPHA * x + jnp.dot(m.astype(BF16), wo_ref[...], preferred_element_type=F32)
        h_ref[...] = _layer_norm(z, g_ref[...], b_ref[...])

    pl.when(i < n_p)(functools.partial(run, xp_ref, yhp_ref, ydp_ref, hp_ref))
    pl.when(i >= n_p)(functools.partial(run, xs_ref, yhs_ref, yds_ref, hs_ref))


def _merge(xp, xs, yhp, yhs, ydp, yds, wg, wh, wd, wo, g, b, tm):
    n_p, n_s = xp.shape[0] // tm, xs.shape[0] // tm
    tile_p, tile_s = _two_group_tiles(tm, n_p)
    wspec = _resident((D_MODEL, D_MODEL))
    vspec = _resident((1, D_MODEL))
    return pl.pallas_call(
        functools.partial(_merge_kernel, n_p=n_p),
        grid=(n_p + n_s,),
        in_specs=([tile_p, tile_s] * 3 + [_resident((D_MODEL, N_GATE_GROUPS * D_MODEL))]
                  + [wspec] * 3 + [vspec] * 2),
        out_specs=[tile_p, tile_s],
        out_shape=[jax.ShapeDtypeStruct(xp.shape, F32), jax.ShapeDtypeStruct(xs.shape, F32)],
        compiler_params=_params(1),
        name="merge",
    )(xp, xs, yhp, yhs, ydp, yds, wg, wh, wd, wo, g, b)


def _ffn_tile(h, halo, put_last, n_seg, wup_ref, cw_ref, cb_ref, wdn_ref, g_ref, b_ref):
    tm = h.shape[0]
    seg = tm // n_seg
    hb = h.astype(BF16)
    slab_row = lax.broadcasted_iota(jnp.int32, (SUBLANES, FF_CHUNK), 0)

    def conv_cols(c0):
        cols = slice(c0, c0 + FF_CHUNK)
        up = jnp.dot(hb, wup_ref[:, cols], preferred_element_type=F32)
        cw = cw_ref[:, cols]
        cb = cb_ref[:, cols]
        body = cb + cw[0:1] * pltpu.roll(up, 2, 0) + cw[1:2] * pltpu.roll(up, 1, 0) + cw[2:3] * up
        pieces = []
        for s in range(n_seg):
            r0 = s * seg
            pv = halo(s, cols)
            slab = up[r0:r0 + SUBLANES]
            tap1 = jnp.where(slab_row == 0, pv[1:2], pltpu.roll(slab, 1, 0))
            tap2 = jnp.where(slab_row == 0, pv[0:1],
                             jnp.where(slab_row == 1, pv[1:2], pltpu.roll(slab, 2, 0)))
            pieces.append(cb + cw[0:1] * tap2 + cw[1:2] * tap1 + cw[2:3] * slab)
            pieces.append(body[r0 + SUBLANES:r0 + seg])
            put_last(s, cols, up[r0 + seg - (CONV_WIDTH - 1):r0 + seg])
        return jnp.concatenate(pieces, axis=0)

    acc = jnp.zeros((tm, D_MODEL), F32)
    for c in range(D_FF // FF_CHUNK):
        val = conv_cols(c * FF_CHUNK)
        gate = conv_cols(D_FF + c * FF_CHUNK)
        act = gate * _sigmoid(gate) * val
        acc = acc + jnp.dot(act.astype(BF16), wdn_ref[c * FF_CHUNK:(c + 1) * FF_CHUNK, :],
                            preferred_element_type=F32)
    return _layer_norm(ALPHA * h + acc, g_ref[...], b_ref[...])


def _ffn_kernel(hp_ref, hs_ref, buf0p_ref, buf0s_ref, wup_ref, cw_ref, cb_ref, wdn_ref, g_ref, b_ref,
                outp_ref, outs_ref, cstp_ref, csts_ref, prev_ref, *, n_p, tiles_per_row, n_seg_s):
    i = pl.program_id(0)
    weights = (wup_ref, cw_ref, cb_ref, wdn_ref, g_ref, b_ref)

    @pl.when(i < n_p)
    def _prompt():
        @pl.when(i % tiles_per_row == 0)
        def _init():
            prev_ref[...] = buf0p_ref[...]

        def put_last(s, cols, rows):
            prev_ref[:, cols] = rows
            cstp_ref[:, cols] = rows

        outp_ref[...] = _ffn_tile(hp_ref[...], lambda s, cols: prev_ref[:, cols], put_last, 1,
                                  *weights)

    @pl.when(i >= n_p)
    def _sample():
        def put_last(s, cols, rows):
            csts_ref[s, :, cols] = rows

        outs_ref[...] = _ffn_tile(hs_ref[...], lambda s, cols: buf0s_ref[s, :, cols], put_last,
                                  n_seg_s, *weights)


def _ffn(hp, hs, buf0p, buf0s, t_p, t_s, wup, cw, cb, wdn, g, b, tm):
    assert t_p % tm == 0 and tm % t_s == 0 and buf0s.shape[0] % (tm // t_s) == 0
    n_p, n_s = hp.shape[0] // tm, hs.shape[0] // tm
    tpr, n_seg_s = t_p // tm, tm // t_s
    tile_p, tile_s = _two_group_tiles(tm, n_p)
    cshape = (CONV_WIDTH - 1, 2 * D_FF)
    cst_p = pl.BlockSpec((None,) + cshape, lambda i: (jnp.minimum(i, n_p - 1) // tpr, 0, 0))
    cst_s = pl.BlockSpec((n_seg_s,) + cshape, lambda i: (jnp.maximum(i - n_p, 0), 0, 0))
    return pl.pallas_call(
        functools.partial(_ffn_kernel, n_p=n_p, tiles_per_row=tpr, n_seg_s=n_seg_s),
        grid=(n_p + n_s,),
        in_specs=[tile_p, tile_s, cst_p, cst_s,
                  _resident((D_MODEL, 2 * D_FF)), _resident((CONV_WIDTH, 2 * D_FF)),
                  _resident((1, 2 * D_FF)), _resident((D_FF, D_MODEL)),
                  _resident((1, D_MODEL)), _resident((1, D_MODEL))],
        out_specs=[tile_p, tile_s, cst_p, cst_s],
        out_shape=[jax.ShapeDtypeStruct(hp.shape, F32), jax.ShapeDtypeStruct(hs.shape, F32),
                   jax.ShapeDtypeStruct(buf0p.shape, F32), jax.ShapeDtypeStruct(buf0s.shape, F32)],
        scratch_shapes=[pltpu.VMEM(cshape, F32)],
        compiler_params=_params(1),
        name="ffn",
    )(hp, hs, buf0p, buf0s, wup, cw, cb, wdn, g, b)


def kernel(x_prompt, x_sample, cache_k, cache_v, state_hgrn, state_ffn_conv, w_in, hg_lb_logits,
           hg_norm_g, da_lambda_q1, da_lambda_k1, da_lambda_q2, da_lambda_k2, da_subln_g,
           w_br_hg, w_br_da, w_out, ln1_g, ln1_b, w_up, conv_w, conv_b, w_down, ln2_g, ln2_b):
    assert w_in.shape[0] == DEPTH == 1 and hg_lb_logits.shape[0] == DEPTH + 1
    l = 0
    bp, tp, _ = x_prompt.shape
    bs, ts, _ = x_sample.shape
    assert tp % ATT_T == 0 and (bp * tp) % ROW_TILE == 0 and (bs * ts) % ROW_TILE == 0

    grp = lambda a, b: w_in[l][:, a * D_MODEL:b * D_MODEL]
    w_mix = jnp.concatenate([grp(0, 3), grp(4, 7)], axis=1).astype(BF16)
    w_gate = jnp.concatenate([grp(3, 4), grp(7, 9)], axis=1).astype(BF16)
    lams = [a[l].reshape(1, DA_HEAD_DIM)
            for a in (da_lambda_q1, da_lambda_k1, da_lambda_q2, da_lambda_k2)]
    lb_logits = hg_lb_logits.astype(F32)
    gn = hg_norm_g[l].reshape(1, HG_DIM)
    lam_init = 0.8 - 0.6 * math.exp(-0.3 * l)

    xp = x_prompt.reshape(bp * tp, D_MODEL)
    xs = x_sample.reshape(bs * ts, D_MODEL)
    mix_p, mix_s = _in_proj(xp, xs, w_mix, ROW_TILE)
    hq_p, hf_p, hi_p, dq_p, dk_p, dv_p = [a.reshape(bp, tp, D_MODEL) for a in mix_p]
    hq_s, hf_s, hi_s, dq_s, dk_s, dv_s = [a.reshape(bs, ts, D_MODEL) for a in mix_s]

    s0_p = jnp.zeros((bp, N_HEADS, HG_DIM, HG_DIM), F32)
    yh_p, st_p = _hgrn(lb_logits, gn, hq_p, hf_p, hi_p, s0_p, min(tp, 512), HG_HPS)
    yh_s, st_s = _hgrn(lb_logits, gn, hq_s, hf_s, hi_s, state_hgrn[l].astype(F32), ts, HG_HPS)
    yd_p = _attn_prompt(lams, da_subln_g[l].reshape(HG_DIM, 1), dq_p, dk_p, dv_p, lam_init)
    yd_s = _attn_sample(lams, da_subln_g[l].reshape(1, HG_DIM), dq_s, dk_s, dv_s,
                        cache_k[l], cache_v[l], lam_init)

    h_p, h_s = _merge(xp, xs, yh_p.reshape(xp.shape), yh_s.reshape(xs.shape),
                      yd_p.reshape(xp.shape), yd_s.reshape(xs.shape), w_gate,
                      w_br_hg[l].astype(BF16), w_br_da[l].astype(BF16), w_out[l].astype(BF16),
                      ln1_g[l].reshape(1, D_MODEL), ln1_b[l].reshape(1, D_MODEL), ROW_TILE)
    buf0_p = jnp.zeros((bp, CONV_WIDTH - 1, 2 * D_FF), F32)
    out_p, out_s, cst_p, cst_s = _ffn(
        h_p, h_s, buf0_p, state_ffn_conv[l].astype(F32), tp, ts,
        w_up[l].astype(BF16), conv_w[l], conv_b[l].reshape(1, 2 * D_FF), w_down[l].astype(BF16),
        ln2_g[l].reshape(1, D_MODEL), ln2_b[l].reshape(1, D_MODEL), ROW_TILE)

    head4 = lambda a, b, t: a.reshape(1, b, t, N_HEADS, HG_DIM)
    return (out_p.reshape(x_prompt.shape), out_s.reshape(x_sample.shape),
            head4(dk_p, bp, tp), head4(dv_p, bp, tp), st_p[None], cst_p[None],
            head4(dk_s, bs, ts), head4(dv_s, bs, ts), st_s[None], cst_s[None])
```

```python
import functools
import math

import jax
import jax.numpy as jnp
from jax import lax
from jax.experimental import pallas as pl
from jax.experimental.pallas import tpu as pltpu

F32 = jnp.float32
BF16 = jnp.bfloat16

D_MODEL = 1024
DEPTH = 1
CHUNK = 64
HG_DIM = 128
N_HEADS = D_MODEL // HG_DIM
DA_HEAD_DIM = 64
D_FF = 128 * ((8 * D_MODEL // 3 + 127) // 128)
CONV_WIDTH = 3
LN_EPS = 1e-5
ALPHA = (2 * DEPTH) ** 0.25
DA_SCALE = DA_HEAD_DIM ** -0.5
N_MIX_GROUPS = 6
N_GATE_GROUPS = 3
NEG = -1e30

VMEM_LIMIT = 56 * 1024 * 1024
SUBLANES = 8
ROW_TILE = 256

HG_C = 64
HG_D = 8
HG_HPS = 8
FF_CHUNK = D_FF
ATT_T = 256
VT_ROWS = HG_DIM + 16
LOG2E = math.log2(math.e)


def _params(n_axes):
    return pltpu.CompilerParams(dimension_semantics=("arbitrary",) * n_axes,
                                vmem_limit_bytes=VMEM_LIMIT)


def _resident(shape):
    nd = len(shape)
    return pl.BlockSpec(shape, lambda *_: (0,) * nd, pipeline_mode=pl.Buffered(1))


def _sigmoid(x):
    return 0.5 * jnp.tanh(0.5 * x) + 0.5


def _layer_norm(z, g, b):
    mu = jnp.mean(z, axis=-1, keepdims=True)
    zc = z - mu
    var = jnp.mean(zc * zc, axis=-1, keepdims=True)
    return zc * lax.rsqrt(var + LN_EPS) * g + b


NT = (((1,), (1,)), ((), ()))
TN = (((0,), (0,)), ((), ()))


def _two_group_tiles(tm, n_p, width=D_MODEL):
    prompt = pl.BlockSpec((tm, width), lambda i: (jnp.minimum(i, n_p - 1), 0))
    sample = pl.BlockSpec((tm, width), lambda i: (jnp.maximum(i - n_p, 0), 0))
    return prompt, sample


def _in_proj_kernel(xp_ref, xs_ref, w_ref, *out_refs, n_p):
    i = pl.program_id(0)

    def run(x_ref, outs):
        xb = x_ref[...].astype(BF16)
        for j, o_ref in enumerate(outs):
            o_ref[...] = jnp.dot(xb, w_ref[:, j * D_MODEL:(j + 1) * D_MODEL],
                                 preferred_element_type=F32)

    pl.when(i < n_p)(functools.partial(run, xp_ref, out_refs[:N_MIX_GROUPS]))
    pl.when(i >= n_p)(functools.partial(run, xs_ref, out_refs[N_MIX_GROUPS:]))


def _in_proj(xp, xs, w_mix, tm):
    n_p, n_s = xp.shape[0] // tm, xs.shape[0] // tm
    tile_p, tile_s = _two_group_tiles(tm, n_p)
    outs = pl.pallas_call(
        functools.partial(_in_proj_kernel, n_p=n_p),
        grid=(n_p + n_s,),
        in_specs=[tile_p, tile_s, _resident((D_MODEL, N_MIX_GROUPS * D_MODEL))],
        out_specs=[tile_p] * N_MIX_GROUPS + [tile_s] * N_MIX_GROUPS,
        out_shape=([jax.ShapeDtypeStruct(xp.shape, F32)] * N_MIX_GROUPS
                   + [jax.ShapeDtypeStruct(xs.shape, F32)] * N_MIX_GROUPS),
        compiler_params=_params(1),
        name="in_proj",
    )(xp, xs, w_mix)
    return outs[:N_MIX_GROUPS], outs[N_MIX_GROUPS:]


def _cumsum_rows(g, row):
    b = g
    sh = 1
    while sh < g.shape[0]:
        b = b + jnp.where(row >= sh, pltpu.roll(b, sh, 0), 0.0)
        sh *= 2
    return b


def _block_rows(x, blk, offset):
    parts = [jnp.broadcast_to(x[j * blk + offset:j * blk + offset + 1], (blk, x.shape[1]))
             for j in range(x.shape[0] // blk)]
    return jnp.concatenate(parts, axis=0)


def _hgrn_chunk(q, hf, v, st, lb, gn, consts):
    row, level_masks, lane_dc, keep_dc = consts
    th = jnp.tanh(0.5 * hf)
    f = lb + (1.0 - lb) * (0.5 * th + 0.5)
    kk = (1.0 - lb) * (0.5 - 0.5 * th)
    b = _cumsum_rows(jnp.log2(f), row)
    b_ex = jnp.where(row == 0, 0.0, pltpu.roll(b, 1, 0))
    w = jnp.log2(kk) - b

    a = None
    for blk, mask in level_masks:
        qs = q * jnp.exp2(b - _block_rows(b_ex, blk, 0))
        ks = jnp.exp2(w + _block_rows(b, blk, blk - 1))
        a_l = lax.dot_general(qs.astype(BF16), ks.astype(BF16), NT, preferred_element_type=F32)
        a = jnp.where(mask, a_l, 0.0) if a is None else jnp.where(mask, a_l, a)
    diag = []
    for j in range(HG_C // HG_D):
        r0 = j * HG_D
        bj = b[r0:r0 + HG_D]
        qj = q[r0:r0 + HG_D]
        dj = jnp.zeros((HG_D, HG_C), F32)
        for sl in range(HG_D):
            s = r0 + sl
            p = qj * jnp.exp2(bj + w[s:s + 1])
            dj = jnp.where(lane_dc == s, jnp.sum(p, axis=1, keepdims=True), dj)
        diag.append(dj)
    a = jnp.where(keep_dc, jnp.concatenate(diag, axis=0), a)

    qh = q * jnp.exp2(b)
    o = (jnp.dot(a.astype(BF16), v.astype(BF16), preferred_element_type=F32)
         + lax.dot_general(qh.astype(BF16), st.astype(BF16), NT, preferred_element_type=F32))
    bl = b[HG_C - 1:HG_C]
    kh = jnp.exp2(w + bl)
    st_new = st * jnp.exp2(bl) + lax.dot_general(v.astype(BF16), kh.astype(BF16), TN,
                                                 preferred_element_type=F32)
    ms = jnp.mean(o * o, axis=1, keepdims=True)
    return o * lax.rsqrt(ms + LN_EPS) * gn, st_new


def _hgrn_kernel(lbl_ref, gn_ref, q_ref, f_ref, v_ref, s0_ref, y_ref, sout_ref, st_ref,
                 *, n_sub, hps):
    t = pl.program_id(2)

    @pl.when(t == 0)
    def _init():
        for hh in range(hps):
            st_ref[hh] = s0_ref[hh].T

    lg = lbl_ref[...]
    e = jnp.exp(lg - jnp.max(lg, axis=0, keepdims=True))
    lb_all = e[0:1, :] / jnp.sum(e, axis=0, keepdims=True)
    gn = gn_ref[...]

    row = lax.broadcasted_iota(jnp.int32, (HG_C, HG_DIM), 0)
    t_cc = lax.broadcasted_iota(jnp.int32, (HG_C, HG_C), 0)
    s_cc = lax.broadcasted_iota(jnp.int32, (HG_C, HG_C), 1)
    level_masks = []
    blk = HG_C // 2
    while blk >= HG_D:
        level_masks.append((blk, ((t_cc // blk) % 2 == 1) & (s_cc // blk == t_cc // blk - 1)))
        blk //= 2
    lane_dc = lax.broadcasted_iota(jnp.int32, (HG_D, HG_C), 1)
    keep_dc = (s_cc // HG_D == t_cc // HG_D) & (s_cc <= t_cc)
    consts = (row, level_masks, lane_dc, keep_dc)

    def chunk(c, carry):
        rows = pl.ds(pl.multiple_of(c * HG_C, HG_C), HG_C)
        for hh in range(hps):
            cols = slice(hh * HG_DIM, (hh + 1) * HG_DIM)
            y, st_new = _hgrn_chunk(q_ref[rows, cols], f_ref[rows, cols], v_ref[rows, cols],
                                    st_ref[hh], lb_all[:, cols], gn, consts)
            st_ref[hh] = st_new
            y_ref[rows, cols] = y
        return carry

    lax.fori_loop(0, n_sub, chunk, 0)

    @pl.when(t == pl.num_programs(2) - 1)
    def _final():
        for hh in range(hps):
            sout_ref[hh] = st_ref[hh].T


def _hgrn(lb_logits, gn, hq, hf, hi, s0, tc, hps):
    bsz, t_len, _ = hq.shape
    assert t_len % tc == 0 and tc % HG_C == 0 and N_HEADS % hps == 0
    tok = pl.BlockSpec((None, tc, hps * HG_DIM), lambda b, h, t: (b, t, h))
    state = pl.BlockSpec((None, hps, HG_DIM, HG_DIM), lambda b, h, t: (b, h, 0, 0))
    return pl.pallas_call(
        functools.partial(_hgrn_kernel, n_sub=tc // HG_C, hps=hps),
        grid=(bsz, N_HEADS // hps, t_len // tc),
        in_specs=[pl.BlockSpec((DEPTH + 1, hps * HG_DIM), lambda b, h, t: (0, h)),
                  pl.BlockSpec((1, HG_DIM), lambda b, h, t: (0, 0)),
                  tok, tok, tok, state],
        out_specs=[tok, state],
        out_shape=[jax.ShapeDtypeStruct(hq.shape, F32),
                   jax.ShapeDtypeStruct(s0.shape, F32)],
        scratch_shapes=[pltpu.VMEM((hps, HG_DIM, HG_DIM), F32)],
        compiler_params=_params(3),
        name="hgrn",
    )(lb_logits, gn, hq, hf, hi, s0)


def _lambda(lq1_ref, lk1_ref, lq2_ref, lk2_ref, lam_init):
    return (jnp.exp(jnp.sum(lq1_ref[...] * lk1_ref[...], axis=1, keepdims=True))
            - jnp.exp(jnp.sum(lq2_ref[...] * lk2_ref[...], axis=1, keepdims=True)) + lam_init)


def _split_maps(q):
    lane = lax.broadcasted_iota(jnp.int32, q.shape, 1)
    return jnp.where(lane < DA_HEAD_DIM, q, 0.0), jnp.where(lane >= DA_HEAD_DIM, q, 0.0)


def _attn_prompt_kernel(lq1_ref, lk1_ref, lq2_ref, lk2_ref, g_ref, q_ref, kn_ref, vn_ref, y_ref,
                        kb_ref, vt_ref, *, n_q, lam_init):
    t = ATT_T
    t_len = kn_ref.shape[0]
    kb_ref[...] = kn_ref[...].astype(BF16)
    vt_ref[0:HG_DIM, :] = vn_ref[...].T.astype(BF16)
    pad_row = lax.broadcasted_iota(jnp.int32, (VT_ROWS - HG_DIM, t_len), 0)
    vt_ref[HG_DIM:VT_ROWS, :] = jnp.where(pad_row == 0, 1.0, 0.0).astype(BF16)

    lam = _lambda(lq1_ref, lk1_ref, lq2_ref, lk2_ref, lam_init)
    key_chunk = lax.broadcasted_iota(jnp.int32, (t, t), 0) // CHUNK
    qry_chunk = lax.broadcasted_iota(jnp.int32, (t, t), 1) // CHUNK
    diag_mask = key_chunk <= qry_chunk

    qm, m, acc, scores = {}, {}, {}, {}

    def score_pass(qi, c):
        if c == 0:
            q = q_ref[qi * t:(qi + 1) * t, :] * (DA_SCALE * LOG2E)
            qm[qi] = [x.astype(BF16) for x in _split_maps(q)]
            m[qi] = [jnp.full((1, t), NEG, F32)] * 2
        kb = kb_ref[c * t:(c + 1) * t, :]
        for mp in range(2):
            s = lax.dot_general(kb, qm[qi][mp], NT,
                                preferred_element_type=F32)
            if c == qi:
                s = jnp.where(diag_mask, s, NEG)
            scores[qi, mp, c] = s
            m[qi] = [jnp.maximum(m[qi][i], jnp.max(s, axis=0, keepdims=True)) if i == mp
                     else m[qi][i] for i in range(2)]

    def value_pass(qi, c):
        if c == 0:
            acc[qi] = [jnp.zeros((VT_ROWS, t), F32)] * 2
        for mp in range(2):
            p = jnp.exp2(scores.pop((qi, mp, c)) - m[qi][mp])
            pv = jnp.dot(vt_ref[:, c * t:(c + 1) * t], p.astype(BF16),
                         preferred_element_type=F32)
            acc[qi] = [acc[qi][i] + pv if i == mp else acc[qi][i] for i in range(2)]
        if c == qi:
            o = [x[0:HG_DIM] / x[HG_DIM:HG_DIM + 1] for x in acc[qi]]
            a = o[0] - lam * o[1]
            ms = jnp.mean(a * a, axis=0, keepdims=True)
            y = a * lax.rsqrt(ms + LN_EPS) * g_ref[...] * (1.0 - lam_init)
            y_ref[qi * t:(qi + 1) * t, :] = y.T

    for qi in range(n_q + 1):
        for c in range(qi + 1):
            if qi < n_q:
                score_pass(qi, c)
            if qi >= 1 and c < qi:
                value_pass(qi - 1, c)


def _attn_prompt(lams, g_col, dq, dk, dv, lam_init):
    bsz, t_len, _ = dq.shape
    vec = pl.BlockSpec((1, DA_HEAD_DIM), lambda b, h: (0, 0))
    seq = pl.BlockSpec((None, t_len, HG_DIM), lambda b, h: (b, 0, h))
    return pl.pallas_call(
        functools.partial(_attn_prompt_kernel, n_q=t_len // ATT_T, lam_init=lam_init),
        grid=(bsz, N_HEADS),
        in_specs=[vec, vec, vec, vec, pl.BlockSpec((HG_DIM, 1), lambda b, h: (0, 0)),
                  seq, seq, seq],
        out_specs=seq,
        out_shape=jax.ShapeDtypeStruct(dq.shape, F32),
        scratch_shapes=[pltpu.VMEM((t_len, HG_DIM), BF16), pltpu.VMEM((VT_ROWS, t_len), BF16)],
        compiler_params=_params(2),
        name="attn_prompt",
    )(*lams, g_col, dq, dk, dv)


def _attn_sample_kernel(lq1_ref, lk1_ref, lq2_ref, lk2_ref, g_ref, q_ref, kn_ref, vn_ref,
                        kp_ref, vp_ref, y_ref, *, lam_init):
    tq = q_ref.shape[0]
    p_len = kp_ref.shape[0] // N_HEADS
    lam = _lambda(lq1_ref, lk1_ref, lq2_ref, lk2_ref, lam_init)
    scores = {}

    def score_stage(h):
        cols = slice(h * HG_DIM, (h + 1) * HG_DIM)
        qs = jnp.concatenate(_split_maps(q_ref[:, cols] * (DA_SCALE * LOG2E)),
                             axis=0).astype(BF16)
        k_past = kp_ref[pl.ds(h, p_len, stride=N_HEADS), :].astype(BF16)
        scores[h] = (lax.dot_general(qs, k_past, NT, preferred_element_type=F32),
                     lax.dot_general(qs, kn_ref[:, cols].astype(BF16), NT,
                                     preferred_element_type=F32))

    def value_stage(h):
        cols = slice(h * HG_DIM, (h + 1) * HG_DIM)
        s_p, s_n = scores.pop(h)
        m = jnp.maximum(jnp.max(s_p, axis=1, keepdims=True), jnp.max(s_n, axis=1, keepdims=True))
        p_p = jnp.exp2(s_p - m)
        p_n = jnp.exp2(s_n - m)
        l = jnp.sum(p_p, axis=1, keepdims=True) + jnp.sum(p_n, axis=1, keepdims=True)
        v_past = vp_ref[pl.ds(h, p_len, stride=N_HEADS), :].astype(BF16)
        acc = (jnp.dot(p_p.astype(BF16), v_past, preferred_element_type=F32)
               + jnp.dot(p_n.astype(BF16), vn_ref[:, cols].astype(BF16),
                         preferred_element_type=F32))
        o = acc / l
        a = o[:tq] - lam * o[tq:]
        ms = jnp.mean(a * a, axis=1, keepdims=True)
        y_ref[:, cols] = a * lax.rsqrt(ms + LN_EPS) * g_ref[...] * (1.0 - lam_init)

    for h in range(N_HEADS + 1):
        if h < N_HEADS:
            score_stage(h)
        if h >= 1:
            value_stage(h - 1)


def _attn_sample(lams, g_row, dq, dk, dv, past_k, past_v, lam_init):
    bsz, t_len, _ = dq.shape
    p_rows = past_k.shape[1] * N_HEADS
    past_k = past_k.reshape(bsz, p_rows, HG_DIM)
    past_v = past_v.reshape(bsz, p_rows, HG_DIM)
    vec = pl.BlockSpec((1, DA_HEAD_DIM), lambda b: (0, 0))
    new = pl.BlockSpec((None, t_len, D_MODEL), lambda b: (b, 0, 0))
    past = pl.BlockSpec((None, p_rows, HG_DIM), lambda b: (b, 0, 0))
    return pl.pallas_call(
        functools.partial(_attn_sample_kernel, lam_init=lam_init),
        grid=(bsz,),
        in_specs=[vec, vec, vec, vec, pl.BlockSpec((1, HG_DIM), lambda b: (0, 0)),
                  new, new, new, past, past],
        out_specs=new,
        out_shape=jax.ShapeDtypeStruct(dq.shape, F32),
        compiler_params=_params(1),
        name="attn_sample",
    )(*lams, g_row, dq, dk, dv, past_k, past_v)


def _merge_kernel(xp_ref, xs_ref, yhp_ref, yhs_ref, ydp_ref, yds_ref, wg_ref, wh_ref, wd_ref, wo_ref,
                  g_ref, b_ref, hp_ref, hs_ref, *, n_p):
    i = pl.program_id(0)

    def run(x_ref, yh_ref, yd_ref, h_ref):
        x = x_ref[...]
        gates = jnp.dot(x.astype(BF16), wg_ref[...], preferred_element_type=F32)
        y_hg = yh_ref[...] * _sigmoid(gates[:, 0:D_MODEL])
        mh = jnp.dot(y_hg.astype(BF16), wh_ref[...], preferred_element_type=F32)
        md = jnp.dot(yd_ref[...].astype(BF16), wd_ref[...], preferred_element_type=F32)
        m = (_sigmoid(gates[:, D_MODEL:2 * D_MODEL]) * mh
             + _sigmoid(gates[:, 2 * D_MODEL:3 * D_MODEL]) * md)
        z = ALPHA * x + jnp.dot(m.astype(BF16), wo_ref[...], preferred_element_type=F32)
        h_ref[...] = _layer_norm(z, g_ref[...], b_ref[...])

    pl.when(i < n_p)(functools.partial(run, xp_ref, yhp_ref, ydp_ref, hp_ref))
    pl.when(i >= n_p)(functools.partial(run, xs_ref, yhs_ref, yds_ref, hs_ref))


def _merge(xp, xs, yhp, yhs, ydp, yds, wg, wh, wd, wo, g, b, tm):
    n_p, n_s = xp.shape[0] // tm, xs.shape[0] // tm
    tile_p, tile_s = _two_group_tiles(tm, n_p)
    wspec = _resident((D_MODEL, D_MODEL))
    vspec = _resident((1, D_MODEL))
    return pl.pallas_call(
        functools.partial(_merge_kernel, n_p=n_p),
        grid=(n_p + n_s,),
        in_specs=([tile_p, tile_s] * 3 + [_resident((D_MODEL, N_GATE_GROUPS * D_MODEL))]
                  + [wspec] * 3 + [vspec] * 2),
        out_specs=[tile_p, tile_s],
        out_shape=[jax.ShapeDtypeStruct(xp.shape, F32), jax.ShapeDtypeStruct(xs.shape, F32)],
        compiler_params=_params(1),
        name="merge",
    )(xp, xs, yhp, yhs, ydp, yds, wg, wh, wd, wo, g, b)


def _ffn_tile(h, halo, put_last, n_seg, wup_ref, cw_ref, cb_ref, wdn_ref, g_ref, b_ref):
    tm = h.shape[0]
    seg = tm // n_seg
    hb = h.astype(BF16)
    slab_row = lax.broadcasted_iota(jnp.int32, (SUBLANES, FF_CHUNK), 0)

    def conv_cols(c0):
        cols = slice(c0, c0 + FF_CHUNK)
        up = jnp.dot(hb, wup_ref[:, cols], preferred_element_type=F32)
        cw = cw_ref[:, cols]
        cb = cb_ref[:, cols]
        body = cb + cw[0:1] * pltpu.roll(up, 2, 0) + cw[1:2] * pltpu.roll(up, 1, 0) + cw[2:3] * up
        pieces = []
        for s in range(n_seg):
            r0 = s * seg
            pv = halo(s, cols)
            slab = up[r0:r0 + SUBLANES]
            tap1 = jnp.where(slab_row == 0, pv[1:2], pltpu.roll(slab, 1, 0))
            tap2 = jnp.where(slab_row == 0, pv[0:1],
                             jnp.where(slab_row == 1, pv[1:2], pltpu.roll(slab, 2, 0)))
            pieces.append(cb + cw[0:1] * tap2 + cw[1:2] * tap1 + cw[2:3] * slab)
            pieces.append(body[r0 + SUBLANES:r0 + seg])
            put_last(s, cols, up[r0 + seg - (CONV_WIDTH - 1):r0 + seg])
        return jnp.concatenate(pieces, axis=0)

    acc = jnp.zeros((tm, D_MODEL), F32)
    for c in range(D_FF // FF_CHUNK):
        val = conv_cols(c * FF_CHUNK)
        gate = conv_cols(D_FF + c * FF_CHUNK)
        act = gate * _sigmoid(gate) * val
        acc = acc + jnp.dot(act.astype(BF16), wdn_ref[c * FF_CHUNK:(c + 1) * FF_CHUNK, :],
                            preferred_element_type=F32)
    return _layer_norm(ALPHA * h + acc, g_ref[...], b_ref[...])


def _ffn_kernel(hp_ref, hs_ref, buf0p_ref, buf0s_ref, wup_ref, cw_ref, cb_ref, wdn_ref, g_ref, b_ref,
                outp_ref, outs_ref, cstp_ref, csts_ref, prev_ref, *, n_p, tiles_per_row, n_seg_s):
    i = pl.program_id(0)
    weights = (wup_ref, cw_ref, cb_ref, wdn_ref, g_ref, b_ref)

    @pl.when(i < n_p)
    def _prompt():
        @pl.when(i % tiles_per_row == 0)
        def _init():
            prev_ref[...] = buf0p_ref[...]

        def put_last(s, cols, rows):
            prev_ref[:, cols] = rows
            cstp_ref[:, cols] = rows

        outp_ref[...] = _ffn_tile(hp_ref[...], lambda s, cols: prev_ref[:, cols], put_last, 1,
                                  *weights)

    @pl.when(i >= n_p)
    def _sample():
        def put_last(s, cols, rows):
            csts_ref[s, :, cols] = rows

        outs_ref[...] = _ffn_tile(hs_ref[...], lambda s, cols: buf0s_ref[s, :, cols], put_last,
                                  n_seg_s, *weights)


def _ffn(hp, hs, buf0p, buf0s, t_p, t_s, wup, cw, cb, wdn, g, b, tm):
    assert t_p % tm == 0 and tm % t_s == 0 and buf0s.shape[0] % (tm // t_s) == 0
    n_p, n_s = hp.shape[0] // tm, hs.shape[0] // tm
    tpr, n_seg_s = t_p // tm, tm // t_s
    tile_p, tile_s = _two_group_tiles(tm, n_p)
    cshape = (CONV_WIDTH - 1, 2 * D_FF)
    cst_p = pl.BlockSpec((None,) + cshape, lambda i: (jnp.minimum(i, n_p - 1) // tpr, 0, 0))
    cst_s = pl.BlockSpec((n_seg_s,) + cshape, lambda i: (jnp.maximum(i - n_p, 0), 0, 0))
    return pl.pallas_call(
        functools.partial(_ffn_kernel, n_p=n_p, tiles_per_row=tpr, n_seg_s=n_seg_s),
        grid=(n_p + n_s,),
        in_specs=[tile_p, tile_s, cst_p, cst_s,
                  _resident((D_MODEL, 2 * D_FF)), _resident((CONV_WIDTH, 2 * D_FF)),
                  _resident((1, 2 * D_FF)), _resident((D_FF, D_MODEL)),
                  _resident((1, D_MODEL)), _resident((1, D_MODEL))],
        out_specs=[tile_p, tile_s, cst_p, cst_s],
        out_shape=[jax.ShapeDtypeStruct(hp.shape, F32), jax.ShapeDtypeStruct(hs.shape, F32),
                   jax.ShapeDtypeStruct(buf0p.shape, F32), jax.ShapeDtypeStruct(buf0s.shape, F32)],
        scratch_shapes=[pltpu.VMEM(cshape, F32)],
        compiler_params=_params(1),
        name="ffn",
    )(hp, hs, buf0p, buf0s, wup, cw, cb, wdn, g, b)


def kernel(x_prompt, x_sample, cache_k, cache_v, state_hgrn, state_ffn_conv, w_in, hg_lb_logits,
           hg_norm_g, da_lambda_q1, da_lambda_k1, da_lambda_q2, da_lambda_k2, da_subln_g,
           w_br_hg, w_br_da, w_out, ln1_g, ln1_b, w_up, conv_w, conv_b, w_down, ln2_g, ln2_b):
    assert w_in.shape[0] == DEPTH == 1 and hg_lb_logits.shape[0] == DEPTH + 1
    l = 0
    bp, tp, _ = x_prompt.shape
    bs, ts, _ = x_sample.shape
    assert tp % ATT_T == 0 and (bp * tp) % ROW_TILE == 0 and (bs * ts) % ROW_TILE == 0

    grp = lambda a, b: w_in[l][:, a * D_MODEL:b * D_MODEL]
    w_mix = jnp.concatenate([grp(0, 3), grp(4, 7)], axis=1).astype(BF16)
    w_gate = jnp.concatenate([grp(3, 4), grp(7, 9)], axis=1).astype(BF16)
    lams = [a[l].reshape(1, DA_HEAD_DIM)
            for a in (da_lambda_q1, da_lambda_k1, da_lambda_q2, da_lambda_k2)]
    lb_logits = hg_lb_logits.astype(F32)
    gn = hg_norm_g[l].reshape(1, HG_DIM)
    lam_init = 0.8 - 0.6 * math.exp(-0.3 * l)

    xp = x_prompt.reshape(bp * tp, D_MODEL)
    xs = x_sample.reshape(bs * ts, D_MODEL)
    mix_p, mix_s = _in_proj(xp, xs, w_mix, ROW_TILE)
    hq_p, hf_p, hi_p, dq_p, dk_p, dv_p = [a.reshape(bp, tp, D_MODEL) for a in mix_p]
    hq_s, hf_s, hi_s, dq_s, dk_s, dv_s = [a.reshape(bs, ts, D_MODEL) for a in mix_s]

    s0_p = jnp.zeros((bp, N_HEADS, HG_DIM, HG_DIM), F32)
    yh_p, st_p = _hgrn(lb_logits, gn, hq_p, hf_p, hi_p, s0_p, min(tp, 512), HG_HPS)
    yh_s, st_s = _hgrn(lb_logits, gn, hq_s, hf_s, hi_s, state_hgrn[l].astype(F32), ts, HG_HPS)
    yd_p = _attn_prompt(lams, da_subln_g[l].reshape(HG_DIM, 1), dq_p, dk_p, dv_p, lam_init)
    yd_s = _attn_sample(lams, da_subln_g[l].reshape(1, HG_DIM), dq_s, dk_s, dv_s,
                        cache_k[l], cache_v[l], lam_init)

    h_p, h_s = _merge(xp, xs, yh_p.reshape(xp.shape), yh_s.reshape(xs.shape),
                      yd_p.reshape(xp.shape), yd_s.reshape(xs.shape), w_gate,
                      w_br_hg[l].astype(BF16), w_br_da[l].astype(BF16), w_out[l].astype(BF16),
                      ln1_g[l].reshape(1, D_MODEL), ln1_b[l].reshape(1, D_MODEL), ROW_TILE)
    buf0_p = jnp.zeros((bp, CONV_WIDTH - 1, 2 * D_FF), F32)
    out_p, out_s, cst_p, cst_s = _ffn(
        h_p, h_s, buf0_p, state_ffn_conv[l].astype(F32), tp, ts,
        w_up[l].astype(BF16), conv_w[l], conv_b[l].reshape(1, 2 * D_FF), w_down[l].astype(BF16),
        ln2_g[l].reshape(1, D_MODEL), ln2_b[l].reshape(1, D_MODEL), ROW_TILE)

    head4 = lambda a, b, t: a.reshape(1, b, t, N_HEADS, HG_DIM)
    return (out_p.reshape(x_prompt.shape), out_s.reshape(x_sample.shape),
            head4(dk_p, bp, tp), head4(dv_p, bp, tp), st_p[None], cst_p[None],
            head4(dk_s, bs, ts), head4(dv_s, bs, ts), st_s[None], cst_s[None])
```
